```python
import math
import jax, jax.numpy as jnp
from jax import lax
import numpy as np

D_MODEL = 4096
BATCH = 1
SEQ = 8192
DEPTH = 1

CHUNK = 64

MIX_WIDTH = D_MODEL
ATT_WIDTH = MIX_WIDTH // 2
ATT_HEAD_DIM = 128
ATT_HEADS = ATT_WIDTH // ATT_HEAD_DIM
ATT_QBLOCK = 128
SSM_WIDTH = MIX_WIDTH - ATT_WIDTH
SSM_HEAD_DIM = 64
SSM_HEADS = SSM_WIDTH // SSM_HEAD_DIM
SSM_STATE = 128
SSM_GROUPS = 8
CONV_WIDTH = 4
CONV_DIM = SSM_WIDTH + 2 * SSM_GROUPS * SSM_STATE
SSD_CHUNK = CHUNK
IN_PROJ_WIDTH = 3 * ATT_WIDTH + SSM_WIDTH + CONV_DIM + SSM_HEADS

MOE_GROUPS = 8
EXPERTS_PER_GROUP = 8
N_EXPERTS = MOE_GROUPS * EXPERTS_PER_GROUP
EXPERT_TOP_K = 2
EXPERT_HIDDEN = 768
MOE_ROW_BLOCK = 128

EPS = 1e-6

kernel_name = "hymba_sb_ssd_hiermoe_block"


def rmsnorm(x, w):
    xf = x.astype(jnp.float32)
    y = xf * lax.rsqrt(jnp.mean(xf * xf, axis=-1, keepdims=True) + EPS)
    return (y * w.astype(jnp.float32)).astype(x.dtype)


def stick_breaking_attention(q, k, v):
    b, s, h, d = q.shape
    scale = 1.0 / math.sqrt(d)
    outs = []
    for i in range(s // ATT_QBLOCK):
        q0 = i * ATT_QBLOCK
        kend = q0 + ATT_QBLOCK
        qb = q[:, q0:kend]
        kb = k[:, :kend]
        vb = v[:, :kend]
        z = jnp.einsum('bqhd,bkhd->bhqk', qb, kb).astype(jnp.float32) * scale
        qpos = q0 + jnp.arange(ATT_QBLOCK)
        kpos = jnp.arange(kend)
        mask = kpos[None, :] < qpos[:, None]
        log_beta = jax.nn.log_sigmoid(z)
        log_1m = jnp.where(mask, jax.nn.log_sigmoid(-z), 0.0)
        rc = lax.cumsum(log_1m, axis=3, reverse=True) - log_1m
        w = jnp.where(mask, jnp.exp(log_beta + rc), 0.0)
        outs.append(jnp.einsum('bhqk,bkhd->bqhd', w.astype(v.dtype), vb))
    return jnp.concatenate(outs, axis=1)


def causal_depthwise_conv(u, w, bias):
    c = u.shape[-1]
    y = lax.conv_general_dilated(
        u, w[:, None, :].astype(u.dtype), window_strides=(1,),
        padding=[(CONV_WIDTH - 1, 0)],
        dimension_numbers=('NWC', 'WIO', 'NWC'),
        feature_group_count=c)
    return y + bias.astype(u.dtype)


def ssd_chunked(x, dt, a, bmat, cmat):
    b, s, h, p = x.shape
    g, n = bmat.shape[2], bmat.shape[3]
    r = h // g
    L = SSD_CHUNK
    c = s // L
    xd = (x.astype(jnp.float32) * dt[..., None]).reshape(b, c, L, g, r, p)
    da = (dt * a).reshape(b, c, L, g, r)
    bc = bmat.astype(jnp.float32).reshape(b, c, L, g, n)
    cc = cmat.astype(jnp.float32).reshape(b, c, L, g, n)
    a_cs = jnp.cumsum(da, axis=2)
    seg = a_cs[:, :, :, None] - a_cs[:, :, None, :]
    tril = jnp.tril(jnp.ones((L, L), dtype=bool))[None, None, :, :, None, None]
    lmat = jnp.exp(jnp.where(tril, seg, -jnp.inf))
    cb = jnp.einsum('bclgn,bcsgn->bcgls', cc, bc)
    y_diag = jnp.einsum('bcgls,bclsgr,bcsgrp->bclgrp', cb, lmat, xd)
    decay = jnp.exp(a_cs[:, :, -1:] - a_cs)
    states = jnp.einsum('bclgn,bclgr,bclgrp->bcgrpn', bc, decay, xd)
    chunk_decay = jnp.exp(a_cs[:, :, -1])

    def step(hprev, inp):
        dec, st = inp
        return dec[..., None, None] * hprev + st, hprev

    init = jnp.zeros((b, g, r, p, n), jnp.float32)
    _, prev = lax.scan(step, init, (jnp.moveaxis(chunk_decay, 1, 0), jnp.moveaxis(states, 1, 0)))
    prev = jnp.moveaxis(prev, 0, 1)
    y_off = jnp.einsum('bclgn,bcgrpn,bclgr->bclgrp', cc, prev, jnp.exp(a_cs))
    return (y_diag + y_off).reshape(b, s, h, p)


def hierarchical_moe(u, w_group, b_group, w_expert, b_expert, w_gate, w_up, w_down):
    b, s, d = u.shape
    t = b * s
    xf = u.reshape(t, d)
    g_prob = jax.nn.softmax((xf @ w_group).astype(jnp.float32) + b_group.astype(jnp.float32), axis=-1)
    g_w, g_idx = lax.top_k(g_prob, 1)
    e_all = jnp.einsum('td,gde->tge', xf, w_expert).astype(jnp.float32) + b_expert.astype(jnp.float32)
    e_logits = e_all[jnp.arange(t), g_idx[:, 0]]
    e_prob = jax.nn.softmax(e_logits, axis=-1)
    e_w, e_idx = lax.top_k(e_prob, EXPERT_TOP_K)
    e_w = e_w / jnp.sum(e_w, axis=-1, keepdims=True)
    weights = g_w * e_w
    expert_id = g_idx * EXPERTS_PER_GROUP + e_idx

    tk = t * EXPERT_TOP_K
    n_blocks = -(-(tk + N_EXPERTS * (MOE_ROW_BLOCK - 1)) // MOE_ROW_BLOCK)
    n_rows = n_blocks * MOE_ROW_BLOCK
    flat_e = expert_id.reshape(-1).astype(jnp.int32)
    flat_t = jnp.repeat(jnp.arange(t, dtype=jnp.int32), EXPERT_TOP_K)
    flat_w = weights.reshape(-1)
    order = jnp.argsort(flat_e)
    se, st, sw = flat_e[order], flat_t[order], flat_w[order]
    counts = jnp.bincount(flat_e, length=N_EXPERTS)
    padded = (counts + MOE_ROW_BLOCK - 1) // MOE_ROW_BLOCK * MOE_ROW_BLOCK
    start = jnp.cumsum(counts) - counts
    pend = jnp.cumsum(padded)
    pstart = pend - padded
    dest = pstart[se] + jnp.arange(tk, dtype=jnp.int32) - start[se]
    row_tok = jnp.full((n_rows,), t, jnp.int32).at[dest].set(st)
    row_w = jnp.zeros((n_rows,), jnp.float32).at[dest].set(sw)
    blk_start = jnp.arange(n_blocks, dtype=jnp.int32) * MOE_ROW_BLOCK
    blk_e = jnp.minimum(jnp.searchsorted(pend, blk_start, side='right'), N_EXPERTS - 1)

    def run_block(args):
        e, tok = args
        xb = xf[jnp.minimum(tok, t - 1)]
        hdn = jax.nn.silu(xb @ w_gate[e]) * (xb @ w_up[e])
        return hdn @ w_down[e]

    yb = lax.map(run_block, (blk_e, row_tok.reshape(n_blocks, MOE_ROW_BLOCK)))
    yb = yb.reshape(n_rows, d) * row_w[:, None].astype(yb.dtype)
    y = jax.ops.segment_sum(yb, row_tok, num_segments=t)
    return y.reshape(b, s, d)


def setup_inputs(seed: int = 0) -> dict:
    key = jax.random.key(seed)
    ks = jax.random.split(key, 24)
    nrm = jax.random.normal
    f32 = jnp.float32
    x = nrm(ks[0], (BATCH, SEQ, D_MODEL), f32)
    norm_mix_w = 1.0 + 0.02 * nrm(ks[1], (D_MODEL,), f32)
    w_in = nrm(ks[2], (D_MODEL, IN_PROJ_WIDTH), f32) * D_MODEL ** -0.5
    conv_w = nrm(ks[3], (CONV_WIDTH, CONV_DIM), f32) * CONV_WIDTH ** -0.5
    conv_b = 0.02 * nrm(ks[4], (CONV_DIM,), f32)
    dt0 = jnp.exp(jax.random.uniform(ks[5], (SSM_HEADS,), f32) * (math.log(0.1) - math.log(0.001)) + math.log(0.001))
    dt_bias = dt0 + jnp.log(-jnp.expm1(-dt0))
    a_log = jnp.log(jax.random.uniform(ks[6], (SSM_HEADS,), f32, minval=1.0, maxval=16.0))
    d_skip = 1.0 + 0.02 * nrm(ks[7], (SSM_HEADS,), f32)
    ssd_norm_w = 1.0 + 0.02 * nrm(ks[8], (SSM_WIDTH,), f32)
    attn_norm_w = 1.0 + 0.02 * nrm(ks[9], (ATT_WIDTH,), f32)
    w_out = nrm(ks[10], (MIX_WIDTH, D_MODEL), f32) * MIX_WIDTH ** -0.5
    norm_ffn_w = 1.0 + 0.02 * nrm(ks[11], (D_MODEL,), f32)
    w_group = nrm(ks[12], (D_MODEL, MOE_GROUPS), f32) * D_MODEL ** -0.5
    b_group = 0.01 * nrm(ks[13], (MOE_GROUPS,), f32)
    w_expert = nrm(ks[14], (MOE_GROUPS, D_MODEL, EXPERTS_PER_GROUP), f32) * D_MODEL ** -0.5
    b_expert = 0.01 * nrm(ks[15], (MOE_GROUPS, EXPERTS_PER_GROUP), f32)
    w_gate = nrm(ks[16], (N_EXPERTS, D_MODEL, EXPERT_HIDDEN), f32) * D_MODEL ** -0.5
    w_up = nrm(ks[17], (N_EXPERTS, D_MODEL, EXPERT_HIDDEN), f32) * D_MODEL ** -0.5
    w_down = nrm(ks[18], (N_EXPERTS, EXPERT_HIDDEN, D_MODEL), f32) * EXPERT_HIDDEN ** -0.5
    norm_final_w = 1.0 + 0.02 * nrm(ks[19], (D_MODEL,), f32)
    return {"x": x, "norm_mix_w": norm_mix_w, "w_in": w_in, "conv_w": conv_w, "conv_b": conv_b,
            "dt_bias": dt_bias, "a_log": a_log, "d_skip": d_skip, "ssd_norm_w": ssd_norm_w,
            "attn_norm_w": attn_norm_w, "w_out": w_out, "norm_ffn_w": norm_ffn_w,
            "w_group": w_group, "b_group": b_group, "w_expert": w_expert, "b_expert": b_expert,
            "w_gate": w_gate, "w_up": w_up, "w_down": w_down, "norm_final_w": norm_final_w}


def reference(x, norm_mix_w, w_in, conv_w, conv_b, dt_bias, a_log, d_skip, ssd_norm_w,
              attn_norm_w, w_out, norm_ffn_w, w_group, b_group, w_expert, b_expert,
              w_gate, w_up, w_down, norm_final_w):
    b, s, _ = x.shape
    h = x
    for _layer in range(DEPTH):
        u = rmsnorm(h, norm_mix_w)
        proj = u @ w_in
        cuts = [ATT_WIDTH, 2 * ATT_WIDTH, 3 * ATT_WIDTH, 3 * ATT_WIDTH + SSM_WIDTH,
                3 * ATT_WIDTH + SSM_WIDTH + CONV_DIM]
        q, k, v, z, xbc, dt_raw = jnp.split(proj, cuts, axis=-1)
        q = q.reshape(b, s, ATT_HEADS, ATT_HEAD_DIM)
        k = k.reshape(b, s, ATT_HEADS, ATT_HEAD_DIM)
        v = v.reshape(b, s, ATT_HEADS, ATT_HEAD_DIM)
        att = stick_breaking_attention(q, k, v)
        att = rmsnorm(att, attn_norm_w.reshape(ATT_HEADS, ATT_HEAD_DIM)).reshape(b, s, ATT_WIDTH)
        xbc = jax.nn.silu(causal_depthwise_conv(xbc, conv_w, conv_b))
        xs, bm, cm = jnp.split(xbc, [SSM_WIDTH, SSM_WIDTH + SSM_GROUPS * SSM_STATE], axis=-1)
        xs = xs.reshape(b, s, SSM_HEADS, SSM_HEAD_DIM)
        bm = bm.reshape(b, s, SSM_GROUPS, SSM_STATE)
        cm = cm.reshape(b, s, SSM_GROUPS, SSM_STATE)
        dt = jax.nn.softplus(dt_raw.astype(jnp.float32) + dt_bias.astype(jnp.float32))
        a = -jnp.exp(a_log.astype(jnp.float32))
        y = ssd_chunked(xs, dt, a, bm, cm) + d_skip.astype(jnp.float32)[:, None] * xs.astype(jnp.float32)
        y = y.reshape(b, s, SSM_WIDTH) * jax.nn.silu(z.astype(jnp.float32))
        y = rmsnorm(y.reshape(b, s, SSM_GROUPS, SSM_WIDTH // SSM_GROUPS),
                    ssd_norm_w.reshape(SSM_GROUPS, SSM_WIDTH // SSM_GROUPS))
        y = y.reshape(b, s, SSM_WIDTH).astype(x.dtype)
        h = h + jnp.concatenate([att, y], axis=-1) @ w_out
        h = h + hierarchical_moe(rmsnorm(h, norm_ffn_w), w_group, b_group, w_expert, b_expert,
                                 w_gate, w_up, w_down)
    return rmsnorm(h, norm_final_w)
```

```python
import functools
import math

import jax
import jax.numpy as jnp
from jax import lax
from jax.experimental import pallas as pl
from jax.experimental.pallas import tpu as pltpu

F32 = jnp.float32
BF16 = jnp.bfloat16
U32 = jnp.uint32
I32 = jnp.int32

D_MODEL = 4096
SEQ = 8192
ATT_WIDTH = 2048
ATT_HEAD_DIM = 128
ATT_HEADS = 16
SSM_WIDTH = 2048
SSM_HEAD_DIM = 64
SSM_HEADS = 32
SSM_STATE = 128
SSM_GROUPS = 8
SSM_GROUP_WIDTH = SSM_WIDTH // SSM_GROUPS
CONV_WIDTH = 4
MOE_GROUPS = 8
EXPERTS_PER_GROUP = 8
N_EXPERTS = 64
EXPERT_TOP_K = 2
EXPERT_HIDDEN = 768
EPS = 1e-6

LANES = 128
SUBLANES = 8
VMEM_LIMIT = 56 * 1024 * 1024

NORM_ROWS = 256
MM_TM = 1024
MM_TN = 1024
ATT_BLK = 128
SSD_T = 128
MOE_BLK = 128
MOE_TH = 384
MOE_NBLK = -(-(SEQ * EXPERT_TOP_K + N_EXPERTS * (MOE_BLK - 1)) // MOE_BLK)
MOE_ROWS = MOE_NBLK * MOE_BLK
DMA_WINDOW = 32

NEG_BIG = -1e30


def _cparams(sem):
    return pltpu.CompilerParams(dimension_semantics=sem, vmem_limit_bytes=VMEM_LIMIT)


def _softplus(x):
    return jnp.maximum(x, 0.0) + jnp.log(1.0 + jnp.exp(-jnp.abs(x)))


def _silu(x):
    return x / (1.0 + jnp.exp(-x))


def _split2(x):
    hi = x.astype(BF16)
    lo = (x - hi.astype(F32)).astype(BF16)
    return hi, lo


def _split3(x):
    h1 = x.astype(BF16)
    r1 = x - h1.astype(F32)
    h2 = r1.astype(BF16)
    h3 = (r1 - h2.astype(F32)).astype(BF16)
    return h1, h2, h3


def _dot(a, b):
    return jnp.dot(a, b, preferred_element_type=F32)


def _dot_nt(a, b):
    return lax.dot_general(a, b, (((1,), (1,)), ((), ())), preferred_element_type=F32)


def _dot_exact01(x, m, parts):
    terms = _split3(x) if parts == 3 else _split2(x)
    acc = _dot(terms[0], m)
    for t in terms[1:]:
        acc = acc + _dot(t, m)
    return acc


def _pack_bf16_pair(a, b):
    ab = pltpu.bitcast(a.astype(BF16).astype(F32), U32)
    bb = pltpu.bitcast(b.astype(BF16).astype(F32), U32)
    return ab | (bb >> 16)


def _unpack_bf16_pair(p):
    a = pltpu.bitcast(p & jnp.uint32(0xFFFF0000), F32)
    b = pltpu.bitcast(p << 16, F32)
    return a, b


def _norm_dt_kernel(x_ref, w_ref, wdt_ref, u_ref, dt_ref):
    x = x_ref[...]
    y = x * lax.rsqrt(jnp.mean(x * x, axis=-1, keepdims=True) + EPS) * w_ref[...]
    ub = y.astype(BF16)
    u_ref[...] = ub
    dt_ref[...] = _dot(ub, wdt_ref[...])


def _norm_dt(x2, norm_w, wdt_b):
    s, d = x2.shape
    return pl.pallas_call(
        _norm_dt_kernel,
        grid=(s // NORM_ROWS,),
        in_specs=[
            pl.BlockSpec((NORM_ROWS, d), lambda i: (i, 0)),
            pl.BlockSpec((1, d), lambda i: (0, 0)),
            pl.BlockSpec((d, LANES), lambda i: (0, 0)),
        ],
        out_specs=[
            pl.BlockSpec((NORM_ROWS, d), lambda i: (i, 0)),
            pl.BlockSpec((NORM_ROWS, LANES), lambda i: (i, 0)),
        ],
        out_shape=[jax.ShapeDtypeStruct((s, d), BF16), jax.ShapeDtypeStruct((s, LANES), F32)],
        compiler_params=_cparams(("parallel",)),
        name="norm_dt",
    )(x2, norm_w.reshape(1, d), wdt_b)


def _proj_heads_kernel(u_ref, w_ref, o_ref):
    acc = _dot(u_ref[...], w_ref[...])
    for hh in range(MM_TN // ATT_HEAD_DIM):
        o_ref[hh] = acc[:, hh * ATT_HEAD_DIM:(hh + 1) * ATT_HEAD_DIM].astype(o_ref.dtype)


def _proj_heads(u, w_b):
    s, d = u.shape
    n = w_b.shape[1]
    hpt = MM_TN // ATT_HEAD_DIM
    return pl.pallas_call(
        _proj_heads_kernel,
        grid=(n // MM_TN, s // MM_TM),
        in_specs=[
            pl.BlockSpec((MM_TM, d), lambda j, i: (i, 0)),
            pl.BlockSpec((d, MM_TN), lambda j, i: (0, j)),
        ],
        out_specs=pl.BlockSpec((hpt, MM_TM, ATT_HEAD_DIM), lambda j, i: (j, i, 0)),
        out_shape=jax.ShapeDtypeStruct((n // ATT_HEAD_DIM, s, ATT_HEAD_DIM), BF16),
        compiler_params=_cparams(("parallel", "parallel")),
        name="proj_qkv",
    )(u, w_b)


def _proj_kernel(u_ref, w_ref, o_ref):
    o_ref[...] = _dot(u_ref[...], w_ref[...]).astype(o_ref.dtype)


def _proj(u, w_b, out_dtype, name):
    s, d = u.shape
    n = w_b.shape[1]
    return pl.pallas_call(
        _proj_kernel,
        grid=(n // MM_TN, s // MM_TM),
        in_specs=[
            pl.BlockSpec((MM_TM, d), lambda j, i: (i, 0)),
            pl.BlockSpec((d, MM_TN), lambda j, i: (0, j)),
        ],
        out_specs=pl.BlockSpec((MM_TM, MM_TN), lambda j, i: (i, j)),
        out_shape=jax.ShapeDtypeStruct((s, n), out_dtype),
        compiler_params=_cparams(("parallel", "parallel")),
        name=name,
    )(u, w_b)


def _attn_kernel(q_ref, k_ref, v_ref, nw_ref, uo_ref, o_ref):
    i = pl.program_id(1)
    q = q_ref[0]
    uo = uo_ref[...]
    scale = 1.0 / math.sqrt(ATT_HEAD_DIM)
    row = lax.broadcasted_iota(I32, (ATT_BLK, ATT_BLK), 0)
    col = lax.broadcasted_iota(I32, (ATT_BLK, ATT_BLK), 1)
    strict = col < row

    def block(j, acc, c, masked):
        off = pl.multiple_of(j * ATT_BLK, ATT_BLK)
        kj = k_ref[0, pl.ds(off, ATT_BLK), :]
        vj = v_ref[0, pl.ds(off, ATT_BLK), :]
        z = _dot_nt(q, kj) * scale
        sp = _softplus(z)
        log_1m = -sp
        log_beta = z - sp
        if masked:
            log_1m = jnp.where(strict, log_1m, 0.0)
        r2 = _dot_exact01(log_1m, uo, 2)
        r = r2[:, :ATT_BLK]
        tot = r2[:, ATT_BLK:]
        w = jnp.exp(log_beta + r + c)
        if masked:
            w = jnp.where(strict, w, 0.0)
        acc = acc + _dot(w.astype(BF16), vj)
        return acc, c + tot

    zeros = jnp.zeros((ATT_BLK, ATT_BLK), F32)
    acc, c = block(i, zeros, zeros, True)

    def body(t, carry):
        return block(i - 1 - t, carry[0], carry[1], False)

    acc, c = lax.fori_loop(0, i, body, (acc, c))
    y = acc * lax.rsqrt(jnp.mean(acc * acc, axis=-1, keepdims=True) + EPS) * nw_ref[0]
    o_ref[...] = y.astype(o_ref.dtype)


def _attention(qkv, attn_norm_w, uo):
    s = qkv.shape[1]
    return pl.pallas_call(
        _attn_kernel,
        grid=(ATT_HEADS, s // ATT_BLK),
        in_specs=[
            pl.BlockSpec((1, ATT_BLK, ATT_HEAD_DIM), lambda h, i: (h, i, 0)),
            pl.BlockSpec((1, s, ATT_HEAD_DIM), lambda h, i: (ATT_HEADS + h, 0, 0)),
            pl.BlockSpec((1, s, ATT_HEAD_DIM), lambda h, i: (2 * ATT_HEADS + h, 0, 0)),
            pl.BlockSpec((1, 1, ATT_HEAD_DIM), lambda h, i: (h, 0, 0)),
            pl.BlockSpec((ATT_BLK, 2 * ATT_BLK), lambda h, i: (0, 0)),
        ],
        out_specs=pl.BlockSpec((ATT_BLK, ATT_HEAD_DIM), lambda h, i: (i, h)),
        out_shape=jax.ShapeDtypeStruct((s, ATT_WIDTH), BF16),
        compiler_params=_cparams(("parallel", "arbitrary")),
        name="sb_attn",
    )(qkv, qkv, qkv, attn_norm_w.reshape(ATT_HEADS, 1, ATT_HEAD_DIM), uo)


def _ssd_kernel(z_ref, xs_ref, bc_ref, dt_ref, cw_ref, cb_ref, dtb_ref, alog_ref, dskip_ref, nw_ref,
                tri_ref, eh_ref, et_ref, o_ref,
                xbuf, bbuf, xc, bcc, acol, acst, dtf, decf, expf, state):
    c = pl.program_id(0)
    t = SSD_T
    w2 = SSM_WIDTH

    @pl.when(c == 0)
    def _():
        xbuf[0:SUBLANES, :] = jnp.zeros((SUBLANES, w2), F32)
        bbuf[0:SUBLANES, :] = jnp.zeros((SUBLANES, w2), F32)
        state[...] = jnp.zeros_like(state)

    xbuf[SUBLANES:SUBLANES + t, :] = xs_ref[...]
    bbuf[SUBLANES:SUBLANES + t, :] = bc_ref[...]
    cw = cw_ref[...]
    cbias = cb_ref[...]
    accx = jnp.zeros((t, w2), F32) + cbias[:, :w2]
    accb = jnp.zeros((t, w2), F32) + cbias[:, w2:]
    for k in range(CONV_WIDTH):
        start = SUBLANES - (CONV_WIDTH - 1) + k
        accx = accx + xbuf[start:start + t, :] * cw[k:k + 1, :w2]
        accb = accb + bbuf[start:start + t, :] * cw[k:k + 1, w2:]
    xc[...] = _silu(accx)
    bcc[...] = _silu(accb)
    xbuf[0:SUBLANES, :] = xs_ref[t - SUBLANES:t, :]
    bbuf[0:SUBLANES, :] = bc_ref[t - SUBLANES:t, :]

    dt = _softplus(dt_ref[...] + dtb_ref[...])
    a = -jnp.exp(alog_ref[...])
    a_cs = _dot_exact01_lhs(tri_ref[...], dt * a)
    a_last = a_cs[t - 1:t, :]
    eh = eh_ref[...]
    dtf[...] = _dot_exact01(dt, eh, 2)
    decf[...] = _dot_exact01(dt * jnp.exp(a_last - a_cs), eh, 2)
    expf[...] = _dot_exact01(jnp.exp(a_cs), eh, 2)
    acol[...] = _dot_exact01(a_cs, et_ref[...], 3)
    acst[...] = a_cs.T

    row = lax.broadcasted_iota(I32, (t, t), 0)
    col = lax.broadcasted_iota(I32, (t, t), 1)
    lower = col <= row
    head_of_lane = lax.broadcasted_iota(I32, (t, SSM_GROUP_WIDTH), 1) // SSM_HEAD_DIM

    def group(g, carry):
        go = pl.multiple_of(g * SSM_GROUP_WIDTH, SSM_GROUP_WIDTH)
        no = pl.multiple_of(g * SSM_STATE, SSM_STATE)
        bg = bcc[:, pl.ds(no, SSM_STATE)]
        cg = bcc[:, pl.ds(SSM_GROUPS * SSM_STATE + no, SSM_STATE)]
        cgb = cg.astype(BF16)
        cb = _dot_nt(cgb, bg.astype(BF16))
        xg = xc[:, pl.ds(go, SSM_GROUP_WIDTH)]
        xd = xg * dtf[:, pl.ds(go, SSM_GROUP_WIDTH)]
        xdec = (xg * decf[:, pl.ds(go, SSM_GROUP_WIDTH)]).astype(BF16)
        st = state[g]
        y = _dot(cgb, st.astype(BF16)) * expf[:, pl.ds(go, SSM_GROUP_WIDTH)]
        for r in range(SSM_GROUP_WIDTH // SSM_HEAD_DIM):
            h = g * (SSM_GROUP_WIDTH // SSM_HEAD_DIM) + r
            ho = pl.multiple_of(h * t, t)
            seg = acol[:, pl.ds(ho, t)] - acst[pl.ds(h, 1), :]
            lmat = jnp.exp(jnp.where(lower, seg, NEG_BIG))
            gm = (cb * lmat).astype(BF16)
            xd_r = jnp.where(head_of_lane == r, xd, 0.0).astype(BF16)
            y = y + _dot(gm, xd_r)
        state[g] = st * expf[t - 1:t, pl.ds(go, SSM_GROUP_WIDTH)] + _dot(bg.T.astype(BF16), xdec)
        y = y + dskip_ref[:, pl.ds(go, SSM_GROUP_WIDTH)] * xg
        y = y * _silu(z_ref[:, pl.ds(go, SSM_GROUP_WIDTH)])
        y = y * lax.rsqrt(jnp.mean(y * y, axis=-1, keepdims=True) + EPS) * nw_ref[:, pl.ds(go, SSM_GROUP_WIDTH)]
        o_ref[:, pl.ds(go, SSM_GROUP_WIDTH)] = y.astype(o_ref.dtype)
        return carry

    lax.fori_loop(0, SSM_GROUPS, group, 0)


def _dot_exact01_lhs(m, x):
    h1, h2, h3 = _split3(x)
    return _dot(m, h1) + _dot(m, h2) + _dot(m, h3)


def _ssd(zxbc, dt_raw, conv_w, conv_b, dt_bias, a_log, d_skip, ssd_norm_w):
    s = zxbc.shape[0]
    t = SSD_T
    w2 = SSM_WIDTH
    pad = LANES - SSM_HEADS
    tri = (jnp.arange(t)[:, None] >= jnp.arange(t)[None, :]).astype(BF16)
    heads = jnp.arange(LANES)[:, None]
    eh = (heads == (jnp.arange(w2)[None, :] // SSM_HEAD_DIM)).astype(BF16)
    et = (heads == (jnp.arange(SSM_HEADS * t)[None, :] // t)).astype(BF16)
    dtb = jnp.pad(dt_bias.astype(F32), (0, pad)).reshape(1, LANES)
    alog = jnp.pad(a_log.astype(F32), (0, pad)).reshape(1, LANES)
    dskip = jnp.repeat(d_skip.astype(F32), SSM_HEAD_DIM).reshape(1, w2)
    const = lambda c: (0, 0)
    return pl.pallas_call(
        _ssd_kernel,
        grid=(s // t,),
        in_specs=[
            pl.BlockSpec((t, w2), lambda c: (c, 0)),
            pl.BlockSpec((t, w2), lambda c: (c, 1)),
            pl.BlockSpec((t, w2), lambda c: (c, 2)),
            pl.BlockSpec((t, LANES), lambda c: (c, 0)),
            pl.BlockSpec((CONV_WIDTH, 2 * w2), const),
            pl.BlockSpec((1, 2 * w2), const),
            pl.BlockSpec((1, LANES), const),
            pl.BlockSpec((1, LANES), const),
            pl.BlockSpec((1, w2), const),
            pl.BlockSpec((1, w2), const),
            pl.BlockSpec((t, t), const),
            pl.BlockSpec((LANES, w2), const),
            pl.BlockSpec((LANES, SSM_HEADS * t), const),
        ],
        out_specs=pl.BlockSpec((t, w2), lambda c: (c, 0)),
        out_shape=jax.ShapeDtypeStruct((s, w2), BF16),
        scratch_shapes=[
            pltpu.VMEM((t + SUBLANES, w2), F32),
            pltpu.VMEM((t + SUBLANES, w2), F32),
            pltpu.VMEM((t, w2), F32),
            pltpu.VMEM((t, w2), F32),
            pltpu.VMEM((t, SSM_HEADS * t), F32),
            pltpu.VMEM((LANES, t), F32),
            pltpu.VMEM((t, w2), F32),
            pltpu.VMEM((t, w2), F32),
            pltpu.VMEM((t, w2), F32),
            pltpu.VMEM((SSM_GROUPS, SSM_STATE, SSM_GROUP_WIDTH), F32),
        ],
        compiler_params=_cparams(("arbitrary",)),
        name="ssd",
    )(zxbc, zxbc, zxbc, dt_raw, conv_w.astype(F32), conv_b.astype(F32).reshape(1, 2 * w2), dtb, alog, dskip,
      ssd_norm_w.astype(F32).reshape(1, w2), tri, eh, et)


def _out_proj_kernel(att_ref, y_ref, w_ref, x_ref, o_ref):
    acc = _dot(att_ref[...], w_ref[0:ATT_WIDTH, :])
    acc = acc + _dot(y_ref[...], w_ref[ATT_WIDTH:, :])
    o_ref[...] = x_ref[...] + acc


def _out_proj(att, y, w_b, x2):
    s, d = x2.shape
    tm, tn = 512, MM_TN
    return pl.pallas_call(
        _out_proj_kernel,
        grid=(d // tn, s // tm),
        in_specs=[
            pl.BlockSpec((tm, ATT_WIDTH), lambda j, i: (i, 0)),
            pl.BlockSpec((tm, SSM_WIDTH), lambda j, i: (i, 0)),
            pl.BlockSpec((ATT_WIDTH + SSM_WIDTH, tn), lambda j, i: (0, j)),
            pl.BlockSpec((tm, tn), lambda j, i: (i, j)),
        ],
        out_specs=pl.BlockSpec((tm, tn), lambda j, i: (i, j)),
        out_shape=jax.ShapeDtypeStruct((s, d), F32),
        compiler_params=_cparams(("parallel", "parallel")),
        name="out_proj",
    )(att, y, w_b, x2)


def _router_kernel(h_ref, w_ref, whi_ref, wlo_ref, b_ref, up_ref, route_ref):
    x = h_ref[...]
    u = x * lax.rsqrt(jnp.mean(x * x, axis=-1, keepdims=True) + EPS) * w_ref[...]
    half = D_MODEL // 2
    up_ref[...] = _pack_bf16_pair(u[:, :half], u[:, half:])
    uh, ul = _split2(u)
    whi = whi_ref[...]
    logits = _dot(uh, whi) + _dot(ul, whi) + _dot(uh, wlo_ref[...]) + b_ref[...]
    lane = lax.broadcasted_iota(I32, logits.shape, 1).astype(F32)
    ninf = -jnp.inf
    nolane = float(LANES)
    gl = jnp.where(lane < MOE_GROUPS, logits, ninf)
    gmax = jnp.max(gl, axis=-1, keepdims=True)
    gsum = jnp.sum(jnp.exp(gl - gmax), axis=-1, keepdims=True)
    g_w = 1.0 / gsum
    g_idx = jnp.min(jnp.where(gl == gmax, lane, nolane), axis=-1, keepdims=True)
    lo_lane = MOE_GROUPS + g_idx * EXPERTS_PER_GROUP
    el = jnp.where((lane >= lo_lane) & (lane < lo_lane + EXPERTS_PER_GROUP), logits, ninf)
    m0 = jnp.max(el, axis=-1, keepdims=True)
    i0 = jnp.min(jnp.where(el == m0, lane, nolane), axis=-1, keepdims=True)
    el2 = jnp.where(lane == i0, ninf, el)
    m1 = jnp.max(el2, axis=-1, keepdims=True)
    i1 = jnp.min(jnp.where(el2 == m1, lane, nolane), axis=-1, keepdims=True)
    tt = jnp.exp(m1 - m0)
    w0 = g_w / (1.0 + tt)
    w1 = g_w * tt / (1.0 + tt)
    e0 = i0 - MOE_GROUPS
    e1 = i1 - MOE_GROUPS
    out = jnp.where(lane == 0, e0, jnp.where(lane == 1, e1, jnp.where(lane == 2, w0, jnp.where(lane == 3, w1, 0.0))))
    route_ref[...] = out


def _router(h1, norm_w, wr_hi, wr_lo, br):
    s, d = h1.shape
    return pl.pallas_call(
        _router_kernel,
        grid=(s // NORM_ROWS,),
        in_specs=[
            pl.BlockSpec((NORM_ROWS, d), lambda i: (i, 0)),
            pl.BlockSpec((1, d), lambda i: (0, 0)),
            pl.BlockSpec((d, LANES), lambda i: (0, 0)),
            pl.BlockSpec((d, LANES), lambda i: (0, 0)),
            pl.BlockSpec((1, LANES), lambda i: (0, 0)),
        ],
        out_specs=[
            pl.BlockSpec((NORM_ROWS, d // 2), lambda i: (i, 0)),
            pl.BlockSpec((NORM_ROWS, LANES), lambda i: (i, 0)),
        ],
        out_shape=[jax.ShapeDtypeStruct((s, d // 2), U32), jax.ShapeDtypeStruct((s, LANES), F32)],
        compiler_params=_cparams(("parallel",)),
        name="router",
    )(h1, norm_w.reshape(1, d), wr_hi, wr_lo, br)


def _gather_rows_kernel(idx_ref, src_ref, dst_ref, sem):
    n = dst_ref.shape[0]

    def copy(r):
        return pltpu.make_async_copy(src_ref.at[pl.ds(idx_ref[r], 1)], dst_ref.at[pl.ds(r, 1)], sem)

    def body(r, carry):
        copy(r).start()

        @pl.when(r >= DMA_WINDOW)
        def _():
            copy(r - DMA_WINDOW).wait()

        return carry

    lax.fori_loop(0, n, body, 0)

    def drain(r, carry):
        copy(r).wait()
        return carry

    lax.fori_loop(n - DMA_WINDOW, n, drain, 0)


def _gather_rows(idx, src, n_out, name):
    return pl.pallas_call(
        _gather_rows_kernel,
        grid_spec=pltpu.PrefetchScalarGridSpec(
            num_scalar_prefetch=1,
            grid=(1,),
            in_specs=[pl.BlockSpec(memory_space=pl.ANY)],
            out_specs=pl.BlockSpec(memory_space=pl.ANY),
            scratch_shapes=[pltpu.SemaphoreType.DMA(())],
        ),
        out_shape=jax.ShapeDtypeStruct((n_out, src.shape[1]), src.dtype),
        compiler_params=_cparams(("arbitrary",)),
        name=name,
    )(idx, src)


def _moe_up_kernel(blk_e_ref, nact_ref, x_ref, wg_ref, wu_ref, h_ref, wgb, wub):
    b = pl.program_id(1)
    e = blk_e_ref[b]
    prev = blk_e_ref[jnp.maximum(b - 1, 0)]

    @pl.when((b == 0) | (e != prev))
    def _():
        wgb[...] = wg_ref[0].astype(BF16)
        wub[...] = wu_ref[0].astype(BF16)

    @pl.when(b < nact_ref[0])
    def _():
        xa, xb = _unpack_bf16_pair(x_ref[...])
        x = jnp.concatenate([xa.astype(BF16), xb.astype(BF16)], axis=-1)
        g = _dot(x, wgb[...])
        u = _dot(x, wub[...])
        h_ref[...] = (_silu(g) * u).astype(h_ref.dtype)


def _moe_up(blk_e, nact, xs, w_gate, w_up):
    nht = EXPERT_HIDDEN // MOE_TH
    half = D_MODEL // 2

    def blk(b, nact_ref):
        return jnp.minimum(b, nact_ref[0] - 1)

    return pl.pallas_call(
        _moe_up_kernel,
        grid_spec=pltpu.PrefetchScalarGridSpec(
            num_scalar_prefetch=2,
            grid=(nht, MOE_NBLK),
            in_specs=[
                pl.BlockSpec((MOE_BLK, half), lambda ht, b, be, na: (blk(b, na), 0)),
                pl.BlockSpec((1, D_MODEL, MOE_TH), lambda ht, b, be, na: (be[b], 0, ht)),
                pl.BlockSpec((1, D_MODEL, MOE_TH), lambda ht, b, be, na: (be[b], 0, ht)),
            ],
            out_specs=pl.BlockSpec((MOE_BLK, MOE_TH), lambda ht, b, be, na: (blk(b, na), ht)),
            scratch_shapes=[pltpu.VMEM((D_MODEL, MOE_TH), BF16), pltpu.VMEM((D_MODEL, MOE_TH), BF16)],
        ),
        out_shape=jax.ShapeDtypeStruct((MOE_ROWS, EXPERT_HIDDEN), BF16),
        compiler_params=_cparams(("arbitrary", "arbitrary")),
        name="moe_up",
    )(blk_e, nact, xs, w_gate, w_up)


def _moe_down_kernel(blk_e_ref, nact_ref, h_ref, wd_ref, y_ref, wdb):
    b = pl.program_id(0)
    e = blk_e_ref[b]
    prev = blk_e_ref[jnp.maximum(b - 1, 0)]

    @pl.when((b == 0) | (e != prev))
    def _():
        wdb[...] = wd_ref[0].astype(BF16)

    @pl.when(b < nact_ref[0])
    def _():
        y = _dot(h_ref[...], wdb[...])
        half = D_MODEL // 2
        y_ref[...] = _pack_bf16_pair(y[:, :half], y[:, half:])


def _moe_down(blk_e, nact, hdn, w_down):
    half = D_MODEL // 2

    def blk(b, nact_ref):
        return jnp.minimum(b, nact_ref[0] - 1)

    return pl.pallas_call(
        _moe_down_kernel,
        grid_spec=pltpu.PrefetchScalarGridSpec(
            num_scalar_prefetch=2,
            grid=(MOE_NBLK,),
            in_specs=[
                pl.BlockSpec((MOE_BLK, EXPERT_HIDDEN), lambda b, be, na: (blk(b, na), 0)),
                pl.BlockSpec((1, EXPERT_HIDDEN, D_MODEL), lambda b, be, na: (be[b], 0, 0)),
            ],
            out_specs=pl.BlockSpec((MOE_BLK, half), lambda b, be, na: (blk(b, na), 0)),
            scratch_shapes=[pltpu.VMEM((EXPERT_HIDDEN, D_MODEL), BF16)],
        ),
        out_shape=jax.ShapeDtypeStruct((MOE_ROWS, half), U32),
        compiler_params=_cparams(("arbitrary",)),
        name="moe_down",
    )(blk_e, nact, hdn, w_down)


def _combine_kernel(h_ref, ya0_ref, ya1_ref, route_ref, w_ref, o_ref):
    route = route_ref[...]
    w0 = route[:, 2:3]
    w1 = route[:, 3:4]
    a0, b0 = _unpack_bf16_pair(ya0_ref[...])
    a1, b1 = _unpack_bf16_pair(ya1_ref[...])
    moe = jnp.concatenate([w0 * a0 + w1 * a1, w0 * b0 + w1 * b1], axis=-1)
    h = h_ref[...] + moe
    o_ref[...] = h * lax.rsqrt(jnp.mean(h * h, axis=-1, keepdims=True) + EPS) * w_ref[...]


def _combine(h1, ya, route, norm_w):
    s, d = h1.shape
    nb = s // NORM_ROWS
    return pl.pallas_call(
        _combine_kernel,
        grid=(nb,),
        in_specs=[
            pl.BlockSpec((NORM_ROWS, d), lambda i: (i, 0)),
            pl.BlockSpec((NORM_ROWS, d // 2), lambda i: (i, 0)),
            pl.BlockSpec((NORM_ROWS, d // 2), lambda i: (i + nb, 0)),
            pl.BlockSpec((NORM_ROWS, LANES), lambda i: (i, 0)),
            pl.BlockSpec((1, d), lambda i: (0, 0)),
        ],
        out_specs=pl.BlockSpec((NORM_ROWS, d), lambda i: (i, 0)),
        out_shape=jax.ShapeDtypeStruct((s, d), F32),
        compiler_params=_cparams(("parallel",)),
        name="combine",
    )(h1, ya, ya, route, norm_w.reshape(1, d))


def _dispatch_plan(route):
    s = route.shape[0]
    flat_e = route[:, :EXPERT_TOP_K].astype(I32).reshape(-1)
    onehot = (flat_e[:, None] == jnp.arange(N_EXPERTS, dtype=I32)[None, :]).astype(I32)
    csum = jnp.cumsum(onehot, axis=0)
    rank = jnp.sum((csum - 1) * onehot, axis=1)
    counts = csum[-1]
    padded = (counts + MOE_BLK - 1) // MOE_BLK * MOE_BLK
    pend = jnp.cumsum(padded)
    pstart = pend - padded
    dest = (pstart[flat_e] + rank).astype(I32)
    flat_t = jnp.repeat(jnp.arange(s, dtype=I32), EXPERT_TOP_K)
    row_tok = jnp.zeros((MOE_ROWS,), I32).at[dest].set(flat_t)
    nact = (pend[-1] // MOE_BLK).astype(I32).reshape(1)
    blk_start = jnp.arange(MOE_NBLK, dtype=I32) * MOE_BLK
    blk_e = jnp.minimum(jnp.searchsorted(pend, blk_start, side='right'), N_EXPERTS - 1).astype(I32)
    last_e = blk_e[jnp.maximum(nact[0] - 1, 0)]
    blk_e = jnp.where(jnp.arange(MOE_NBLK) < nact[0], blk_e, last_e)
    return dest, row_tok, blk_e, nact


def kernel(x, norm_mix_w, w_in, conv_w, conv_b, dt_bias, a_log, d_skip, ssd_norm_w, attn_norm_w, w_out,
           norm_ffn_w, w_group, b_group, w_expert, b_expert, w_gate, w_up, w_down, norm_final_w):
    b, s, d = x.shape
    x2 = x.reshape(b * s, d)
    n_qkv = 3 * ATT_WIDTH
    n_zxbc = SSM_WIDTH + SSM_WIDTH + 2 * SSM_GROUPS * SSM_STATE

    w_qkv = w_in[:, :n_qkv].astype(BF16)
    w_zxbc = w_in[:, n_qkv:n_qkv + n_zxbc].astype(BF16)
    w_dt = jnp.pad(w_in[:, n_qkv + n_zxbc:], ((0, 0), (0, LANES - SSM_HEADS))).astype(BF16)
    w_out_b = w_out.astype(BF16)
    wr = jnp.concatenate(
        [w_group, jnp.transpose(w_expert, (1, 0, 2)).reshape(d, N_EXPERTS),
         jnp.zeros((d, LANES - MOE_GROUPS - N_EXPERTS), F32)], axis=1)
    wr_hi = wr.astype(BF16)
    wr_lo = (wr - wr_hi.astype(F32)).astype(BF16)
    br = jnp.concatenate([b_group, b_expert.reshape(-1), jnp.zeros((LANES - MOE_GROUPS - N_EXPERTS,), F32)]).reshape(1, LANES)
    kk = jnp.arange(ATT_BLK)
    uo = jnp.concatenate([(kk[:, None] > kk[None, :]).astype(BF16), jnp.ones((ATT_BLK, ATT_BLK), BF16)], axis=1)

    u, dt_raw = _norm_dt(x2, norm_mix_w, w_dt)
    qkv = _proj_heads(u, w_qkv)
    zxbc = _proj(u, w_zxbc, F32, "proj_zxbc")
    att = _attention(qkv, attn_norm_w, uo)
    y = _ssd(zxbc, dt_raw, conv_w, conv_b, dt_bias, a_log, d_skip, ssd_norm_w)
    h1 = _out_proj(att, y, w_out_b, x2)

    up, route = _router(h1, norm_ffn_w, wr_hi, wr_lo, br)
    dest, row_tok, blk_e, nact = _dispatch_plan(route)
    xs = _gather_rows(row_tok, up, MOE_ROWS, "dispatch")
    hdn = _moe_up(blk_e, nact, xs, w_gate, w_up)
    ys = _moe_down(blk_e, nact, hdn, w_down)
    slot_major = dest.reshape(s, EXPERT_TOP_K).T.reshape(-1)
    ya = _gather_rows(slot_major, ys, s * EXPERT_TOP_K, "undispatch")
    out = _combine(h1, ya, route, norm_final_w)
    return out.reshape(b, s, d)
```

```python
import functools
import math

import jax
import jax.numpy as jnp
from jax import lax
from jax.experimental import pallas as pl
from jax.experimental.pallas import tpu as pltpu

F32 = jnp.float32
BF16 = jnp.bfloat16
U32 = jnp.uint32
I32 = jnp.int32

D_MODEL = 4096
SEQ = 8192
ATT_WIDTH = 2048
ATT_HEAD_DIM = 128
ATT_HEADS = 16
SSM_WIDTH = 2048
SSM_HEAD_DIM = 64
SSM_HEADS = 32
SSM_STATE = 128
SSM_GROUPS = 8
SSM_GROUP_WIDTH = SSM_WIDTH // SSM_GROUPS
CONV_WIDTH = 4
MOE_GROUPS = 8
EXPERTS_PER_GROUP = 8
N_EXPERTS = 64
EXPERT_TOP_K = 2
EXPERT_HIDDEN = 768
EPS = 1e-6

LANES = 128
SUBLANES = 8
VMEM_LIMIT = 56 * 1024 * 1024

NORM_ROWS = 256
MM_TM = 1024
MM_TN = 1024
ATT_BLK = 128
SSD_T = 128
MOE_BLK = 128
MOE_TH = 384
MOE_NBLK = -(-(SEQ * EXPERT_TOP_K + N_EXPERTS * (MOE_BLK - 1)) // MOE_BLK)
MOE_ROWS = MOE_NBLK * MOE_BLK
GATHER_BLK = 512
GATHER_UNROLL = 8
ATT_UNDERFLOW = 104.5

NEG_BIG = -1e30


def _cparams(sem):
    return pltpu.CompilerParams(dimension_semantics=sem, vmem_limit_bytes=VMEM_LIMIT)


def _softplus(x):
    return jnp.maximum(x, 0.0) + jnp.log(1.0 + jnp.exp(-jnp.abs(x)))


def _silu(x):
    return x / (1.0 + jnp.exp(-x))


def _split2(x):
    hi = x.astype(BF16)
    lo = (x - hi.astype(F32)).astype(BF16)
    return hi, lo


def _split3(x):
    h1 = x.astype(BF16)
    r1 = x - h1.astype(F32)
    h2 = r1.astype(BF16)
    h3 = (r1 - h2.astype(F32)).astype(BF16)
    return h1, h2, h3


def _dot(a, b):
    return jnp.dot(a, b, preferred_element_type=F32)


def _dot_nt(a, b):
    return lax.dot_general(a, b, (((1,), (1,)), ((), ())), preferred_element_type=F32)


def _dot_exact01(x, m, parts):
    terms = _split3(x) if parts == 3 else _split2(x)
    acc = _dot(terms[0], m)
    for t in terms[1:]:
        acc = acc + _dot(t, m)
    return acc


def _pack_bf16_pair(a, b):
    ab = pltpu.bitcast(a.astype(BF16).astype(F32), U32)
    bb = pltpu.bitcast(b.astype(BF16).astype(F32), U32)
    return ab | (bb >> 16)


def _unpack_bf16_pair(p):
    a = pltpu.bitcast(p & jnp.uint32(0xFFFF0000), F32)
    b = pltpu.bitcast(p << 16, F32)
    return a, b


def _norm_dt_kernel(x_ref, w_ref, wdt_ref, u_ref, dt_ref):
    x = x_ref[...]
    y = x * lax.rsqrt(jnp.mean(x * x, axis=-1, keepdims=True) + EPS) * w_ref[...]
    ub = y.astype(BF16)
    u_ref[...] = ub
    dt_ref[...] = _dot(ub, wdt_ref[...])


def _norm_dt(x2, norm_w, wdt_b):
    s, d = x2.shape
    return pl.pallas_call(
        _norm_dt_kernel,
        grid=(s // NORM_ROWS,),
        in_specs=[
            pl.BlockSpec((NORM_ROWS, d), lambda i: (i, 0)),
            pl.BlockSpec((1, d), lambda i: (0, 0)),
            pl.BlockSpec((d, LANES), lambda i: (0, 0)),
        ],
        out_specs=[
            pl.BlockSpec((NORM_ROWS, d), lambda i: (i, 0)),
            pl.BlockSpec((NORM_ROWS, LANES), lambda i: (i, 0)),
        ],
        out_shape=[jax.ShapeDtypeStruct((s, d), BF16), jax.ShapeDtypeStruct((s, LANES), F32)],
        compiler_params=_cparams(("parallel",)),
        name="norm_dt",
    )(x2, norm_w.reshape(1, d), wdt_b)


def _proj_heads_kernel(u_ref, w_ref, o_ref):
    acc = _dot(u_ref[...], w_ref[...])
    for hh in range(MM_TN // ATT_HEAD_DIM):
        o_ref[hh] = acc[:, hh * ATT_HEAD_DIM:(hh + 1) * ATT_HEAD_DIM].astype(o_ref.dtype)


def _proj_heads(u, w_b):
    s, d = u.shape
    n = w_b.shape[1]
    hpt = MM_TN // ATT_HEAD_DIM
    return pl.pallas_call(
        _proj_heads_kernel,
        grid=(n // MM_TN, s // MM_TM),
        in_specs=[
            pl.BlockSpec((MM_TM, d), lambda j, i: (i, 0)),
            pl.BlockSpec((d, MM_TN), lambda j, i: (0, j)),
        ],
        out_specs=pl.BlockSpec((hpt, MM_TM, ATT_HEAD_DIM), lambda j, i: (j, i, 0)),
        out_shape=jax.ShapeDtypeStruct((n // ATT_HEAD_DIM, s, ATT_HEAD_DIM), BF16),
        compiler_params=_cparams(("parallel", "parallel")),
        name="proj_qkv",
    )(u, w_b)


def _proj_kernel(u_ref, w_ref, o_ref):
    o_ref[...] = _dot(u_ref[...], w_ref[...]).astype(o_ref.dtype)


def _proj(u, w_b, out_dtype, name):
    s, d = u.shape
    n = w_b.shape[1]
    return pl.pallas_call(
        _proj_kernel,
        grid=(n // MM_TN, s // MM_TM),
        in_specs=[
            pl.BlockSpec((MM_TM, d), lambda j, i: (i, 0)),
            pl.BlockSpec((d, MM_TN), lambda j, i: (0, j)),
        ],
        out_specs=pl.BlockSpec((MM_TM, MM_TN), lambda j, i: (i, j)),
        out_shape=jax.ShapeDtypeStruct((s, n), out_dtype),
        compiler_params=_cparams(("parallel", "parallel")),
        name=name,
    )(u, w_b)


def _attn_kernel(q_ref, k_ref, v_ref, nw_ref, uo_ref, o_ref):
    i = pl.program_id(1)
    q = q_ref[0]
    uo = uo_ref[...]
    scale = 1.0 / math.sqrt(ATT_HEAD_DIM)
    row = lax.broadcasted_iota(I32, (ATT_BLK, ATT_BLK), 0)
    col = lax.broadcasted_iota(I32, (ATT_BLK, ATT_BLK), 1)
    strict = col < row

    def block(j, acc, c, masked):
        off = pl.multiple_of(j * ATT_BLK, ATT_BLK)
        kj = k_ref[0, pl.ds(off, ATT_BLK), :]
        vj = v_ref[0, pl.ds(off, ATT_BLK), :]
        z = _dot_nt(q, kj) * scale
        sp = _softplus(z)
        log_1m = -sp
        log_beta = z - sp
        if masked:
            log_1m = jnp.where(strict, log_1m, 0.0)
        r2 = _dot_exact01(log_1m, uo, 2)
        r = r2[:, :ATT_BLK]
        tot = r2[:, ATT_BLK:]
        w = jnp.exp(log_beta + r + c)
        if masked:
            w = jnp.where(strict, w, 0.0)
        acc = acc + _dot(w.astype(BF16), vj)
        return acc, c + tot

    zeros = jnp.zeros((ATT_BLK, ATT_BLK), F32)
    acc, c = block(i, zeros, zeros, True)

    def live(c):
        return jnp.max(c) > -ATT_UNDERFLOW

    def cond(carry):
        return (carry[0] < i) & carry[3]

    def body(carry):
        t = carry[0]
        acc, c = block(i - 1 - t, carry[1], carry[2], False)
        return t + 1, acc, c, live(c)

    _, acc, c, _ = lax.while_loop(cond, body, (jnp.int32(0), acc, c, live(c)))
    y = acc * lax.rsqrt(jnp.mean(acc * acc, axis=-1, keepdims=True) + EPS) * nw_ref[0]
    o_ref[...] = y.astype(o_ref.dtype)


def _attention(qkv, attn_norm_w, uo):
    s = qkv.shape[1]
    return pl.pallas_call(
        _attn_kernel,
        grid=(ATT_HEADS, s // ATT_BLK),
        in_specs=[
            pl.BlockSpec((1, ATT_BLK, ATT_HEAD_DIM), lambda h, i: (h, i, 0)),
            pl.BlockSpec((1, s, ATT_HEAD_DIM), lambda h, i: (ATT_HEADS + h, 0, 0)),
            pl.BlockSpec((1, s, ATT_HEAD_DIM), lambda h, i: (2 * ATT_HEADS + h, 0, 0)),
            pl.BlockSpec((1, 1, ATT_HEAD_DIM), lambda h, i: (h, 0, 0)),
            pl.BlockSpec((ATT_BLK, 2 * ATT_BLK), lambda h, i: (0, 0)),
        ],
        out_specs=pl.BlockSpec((ATT_BLK, ATT_HEAD_DIM), lambda h, i: (i, h)),
        out_shape=jax.ShapeDtypeStruct((s, ATT_WIDTH), BF16),
        compiler_params=_cparams(("parallel", "arbitrary")),
        name="sb_attn",
    )(qkv, qkv, qkv, attn_norm_w.reshape(ATT_HEADS, 1, ATT_HEAD_DIM), uo)


def _ssd_kernel(z_ref, xs_ref, bc_ref, dt_ref, cw_ref, cb_ref, dtb_ref, alog_ref, dskip_ref, nw_ref,
                tri_ref, eh_ref, et_ref, o_ref,
                xbuf, bbuf, xc, bcc, acol, acst, dtf, decf, expf, state):
    c = pl.program_id(0)
    t = SSD_T
    w2 = SSM_WIDTH

    @pl.when(c == 0)
    def _():
        xbuf[0:SUBLANES, :] = jnp.zeros((SUBLANES, w2), F32)
        bbuf[0:SUBLANES, :] = jnp.zeros((SUBLANES, w2), F32)
        state[...] = jnp.zeros_like(state)

    xbuf[SUBLANES:SUBLANES + t, :] = xs_ref[...]
    bbuf[SUBLANES:SUBLANES + t, :] = bc_ref[...]
    cw = cw_ref[...]
    cbias = cb_ref[...]
    accx = jnp.zeros((t, w2), F32) + cbias[:, :w2]
    accb = jnp.zeros((t, w2), F32) + cbias[:, w2:]
    for k in range(CONV_WIDTH):
        start = SUBLANES - (CONV_WIDTH - 1) + k
        accx = accx + xbuf[start:start + t, :] * cw[k:k + 1, :w2]
        accb = accb + bbuf[start:start + t, :] * cw[k:k + 1, w2:]
    xc[...] = _silu(accx)
    bcc[...] = _silu(accb)
    xbuf[0:SUBLANES, :] = xs_ref[t - SUBLANES:t, :]
    bbuf[0:SUBLANES, :] = bc_ref[t - SUBLANES:t, :]

    dt = _softplus(dt_ref[...] + dtb_ref[...])
    a = -jnp.exp(alog_ref[...])
    a_cs = _dot_exact01_lhs(tri_ref[...], dt * a)
    a_last = a_cs[t - 1:t, :]
    eh = eh_ref[...]
    dtf[...] = _dot_exact01(dt, eh, 2)
    decf[...] = _dot_exact01(dt * jnp.exp(a_last - a_cs), eh, 2)
    expf[...] = _dot_exact01(jnp.exp(a_cs), eh, 2)
    acol[...] = _dot_exact01(a_cs, et_ref[...], 3)
    acst[...] = a_cs.T

    row = lax.broadcasted_iota(I32, (t, t), 0)
    col = lax.broadcasted_iota(I32, (t, t), 1)
    lower = col <= row
    head_of_lane = lax.broadcasted_iota(I32, (t, SSM_GROUP_WIDTH), 1) // SSM_HEAD_DIM

    def group(g, carry):
        go = pl.multiple_of(g * SSM_GROUP_WIDTH, SSM_GROUP_WIDTH)
        no = pl.multiple_of(g * SSM_STATE, SSM_STATE)
        bg = bcc[:, pl.ds(no, SSM_STATE)]
        cg = bcc[:, pl.ds(SSM_GROUPS * SSM_STATE + no, SSM_STATE)]
        cgb = cg.astype(BF16)
        cb = _dot_nt(cgb, bg.astype(BF16))
        xg = xc[:, pl.ds(go, SSM_GROUP_WIDTH)]
        xd = xg * dtf[:, pl.ds(go, SSM_GROUP_WIDTH)]
        xdec = (xg * decf[:, pl.ds(go, SSM_GROUP_WIDTH)]).astype(BF16)
        st = state[g]
        y = _dot(cgb, st.astype(BF16)) * expf[:, pl.ds(go, SSM_GROUP_WIDTH)]
        for r in range(SSM_GROUP_WIDTH // SSM_HEAD_DIM):
            h = g * (SSM_GROUP_WIDTH // SSM_HEAD_DIM) + r
            ho = pl.multiple_of(h * t, t)
            seg = acol[:, pl.ds(ho, t)] - acst[pl.ds(h, 1), :]
            lmat = jnp.exp(jnp.where(lower, seg, NEG_BIG))
            gm = (cb * lmat).astype(BF16)
            xd_r = jnp.where(head_of_lane == r, xd, 0.0).astype(BF16)
            y = y + _dot(gm, xd_r)
        state[g] = st * expf[t - 1:t, pl.ds(go, SSM_GROUP_WIDTH)] + _dot(bg.T.astype(BF16), xdec)
        y = y + dskip_ref[:, pl.ds(go, SSM_GROUP_WIDTH)] * xg
        y = y * _silu(z_ref[:, pl.ds(go, SSM_GROUP_WIDTH)])
        y = y * lax.rsqrt(jnp.mean(y * y, axis=-1, keepdims=True) + EPS) * nw_ref[:, pl.ds(go, SSM_GROUP_WIDTH)]
        o_ref[:, pl.ds(go, SSM_GROUP_WIDTH)] = y.astype(o_ref.dtype)
        return carry

    lax.fori_loop(0, SSM_GROUPS, group, 0)


def _dot_exact01_lhs(m, x):
    h1, h2, h3 = _split3(x)
    return _dot(m, h1) + _dot(m, h2) + _dot(m, h3)


def _ssd(zxbc, dt_raw, conv_w, conv_b, dt_bias, a_log, d_skip, ssd_norm_w):
    s = zxbc.shape[0]
    t = SSD_T
    w2 = SSM_WIDTH
    pad = LANES - SSM_HEADS
    tri = (jnp.arange(t)[:, None] >= jnp.arange(t)[None, :]).astype(BF16)
    heads = jnp.arange(LANES)[:, None]
    eh = (heads == (jnp.arange(w2)[None, :] // SSM_HEAD_DIM)).astype(BF16)
    et = (heads == (jnp.arange(SSM_HEADS * t)[None, :] // t)).astype(BF16)
    dtb = jnp.pad(dt_bias.astype(F32), (0, pad)).reshape(1, LANES)
    alog = jnp.pad(a_log.astype(F32), (0, pad)).reshape(1, LANES)
    dskip = jnp.repeat(d_skip.astype(F32), SSM_HEAD_DIM).reshape(1, w2)
    const = lambda c: (0, 0)
    return pl.pallas_call(
        _ssd_kernel,
        grid=(s // t,),
        in_specs=[
            pl.BlockSpec((t, w2), lambda c: (c, 0)),
            pl.BlockSpec((t, w2), lambda c: (c, 1)),
            pl.BlockSpec((t, w2), lambda c: (c, 2)),
            pl.BlockSpec((t, LANES), lambda c: (c, 0)),
            pl.BlockSpec((CONV_WIDTH, 2 * w2), const),
            pl.BlockSpec((1, 2 * w2), const),
            pl.BlockSpec((1, LANES), const),
            pl.BlockSpec((1, LANES), const),
            pl.BlockSpec((1, w2), const),
            pl.BlockSpec((1, w2), const),
            pl.BlockSpec((t, t), const),
            pl.BlockSpec((LANES, w2), const),
            pl.BlockSpec((LANES, SSM_HEADS * t), const),
        ],
        out_specs=pl.BlockSpec((t, w2), lambda c: (c, 0)),
        out_shape=jax.ShapeDtypeStruct((s, w2), BF16),
        scratch_shapes=[
            pltpu.VMEM((t + SUBLANES, w2), F32),
            pltpu.VMEM((t + SUBLANES, w2), F32),
            pltpu.VMEM((t, w2), F32),
            pltpu.VMEM((t, w2), F32),
            pltpu.VMEM((t, SSM_HEADS * t), F32),
            pltpu.VMEM((LANES, t), F32),
            pltpu.VMEM((t, w2), F32),
            pltpu.VMEM((t, w2), F32),
            pltpu.VMEM((t, w2), F32),
            pltpu.VMEM((SSM_GROUPS, SSM_STATE, SSM_GROUP_WIDTH), F32),
        ],
        compiler_params=_cparams(("arbitrary",)),
        name="ssd",
    )(zxbc, zxbc, zxbc, dt_raw, conv_w.astype(F32), conv_b.astype(F32).reshape(1, 2 * w2), dtb, alog, dskip,
      ssd_norm_w.astype(F32).reshape(1, w2), tri, eh, et)


def _out_proj_kernel(att_ref, y_ref, w_ref, x_ref, o_ref):
    acc = _dot(att_ref[...], w_ref[0:ATT_WIDTH, :])
    acc = acc + _dot(y_ref[...], w_ref[ATT_WIDTH:, :])
    o_ref[...] = x_ref[...] + acc


def _out_proj(att, y, w_b, x2):
    s, d = x2.shape
    tm, tn = 512, MM_TN
    return pl.pallas_call(
        _out_proj_kernel,
        grid=(d // tn, s // tm),
        in_specs=[
            pl.BlockSpec((tm, ATT_WIDTH), lambda j, i: (i, 0)),
            pl.BlockSpec((tm, SSM_WIDTH), lambda j, i: (i, 0)),
            pl.BlockSpec((ATT_WIDTH + SSM_WIDTH, tn), lambda j, i: (0, j)),
            pl.BlockSpec((tm, tn), lambda j, i: (i, j)),
        ],
        out_specs=pl.BlockSpec((tm, tn), lambda j, i: (i, j)),
        out_shape=jax.ShapeDtypeStruct((s, d), F32),
        compiler_params=_cparams(("parallel", "parallel")),
        name="out_proj",
    )(att, y, w_b, x2)


def _router_kernel(h_ref, w_ref, whi_ref, wlo_ref, b_ref, up_ref, route_ref):
    x = h_ref[...]
    u = x * lax.rsqrt(jnp.mean(x * x, axis=-1, keepdims=True) + EPS) * w_ref[...]
    half = D_MODEL // 2
    up_ref[...] = _pack_bf16_pair(u[:, :half], u[:, half:])
    uh, ul = _split2(u)
    whi = whi_ref[...]
    logits = _dot(uh, whi) + _dot(ul, whi) + _dot(uh, wlo_ref[...]) + b_ref[...]
    lane = lax.broadcasted_iota(I32, logits.shape, 1).astype(F32)
    ninf = -jnp.inf
    nolane = float(LANES)
    gl = jnp.where(lane < MOE_GROUPS, logits, ninf)
    gmax = jnp.max(gl, axis=-1, keepdims=True)
    gsum = jnp.sum(jnp.exp(gl - gmax), axis=-1, keepdims=True)
    g_w = 1.0 / gsum
    g_idx = jnp.min(jnp.where(gl == gmax, lane, nolane), axis=-1, keepdims=True)
    lo_lane = MOE_GROUPS + g_idx * EXPERTS_PER_GROUP
    el = jnp.where((lane >= lo_lane) & (lane < lo_lane + EXPERTS_PER_GROUP), logits, ninf)
    m0 = jnp.max(el, axis=-1, keepdims=True)
    i0 = jnp.min(jnp.where(el == m0, lane, nolane), axis=-1, keepdims=True)
    el2 = jnp.where(lane == i0, ninf, el)
    m1 = jnp.max(el2, axis=-1, keepdims=True)
    i1 = jnp.min(jnp.where(el2 == m1, lane, nolane), axis=-1, keepdims=True)
    tt = jnp.exp(m1 - m0)
    w0 = g_w / (1.0 + tt)
    w1 = g_w * tt / (1.0 + tt)
    e0 = i0 - MOE_GROUPS
    e1 = i1 - MOE_GROUPS
    out = jnp.where(lane == 0, e0, jnp.where(lane == 1, e1, jnp.where(lane == 2, w0, jnp.where(lane == 3, w1, 0.0))))
    route_ref[...] = out


def _router(h1, norm_w, wr_hi, wr_lo, br):
    s, d = h1.shape
    return pl.pallas_call(
        _router_kernel,
        grid=(s // NORM_ROWS,),
        in_specs=[
            pl.BlockSpec((NORM_ROWS, d), lambda i: (i, 0)),
            pl.BlockSpec((1, d), lambda i: (0, 0)),
            pl.BlockSpec((d, LANES), lambda i: (0, 0)),
            pl.BlockSpec((d, LANES), lambda i: (0, 0)),
            pl.BlockSpec((1, LANES), lambda i: (0, 0)),
        ],
        out_specs=[
            pl.BlockSpec((NORM_ROWS, d // 2), lambda i: (i, 0)),
            pl.BlockSpec((NORM_ROWS, LANES), lambda i: (i, 0)),
        ],
        out_shape=[jax.ShapeDtypeStruct((s, d // 2), U32), jax.ShapeDtypeStruct((s, LANES), F32)],
        compiler_params=_cparams(("parallel",)),
        name="router",
    )(h1, norm_w.reshape(1, d), wr_hi, wr_lo, br)


def _gather_rows_kernel(idx_ref, src_ref, o_ref, sem):
    base = pl.program_id(0) * GATHER_BLK

    def copy(r):
        return pltpu.make_async_copy(src_ref.at[pl.ds(idx_ref[base + r], 1)], o_ref.at[pl.ds(r, 1)], sem)

    def issue(r, carry):
        copy(r).start()
        return carry

    lax.fori_loop(0, GATHER_BLK, issue, 0, unroll=GATHER_UNROLL)

    def drain(r, carry):
        copy(r).wait()
        return carry

    lax.fori_loop(0, GATHER_BLK, drain, 0, unroll=GATHER_UNROLL)


def _gather_rows(idx, src, n_out, name):
    return pl.pallas_call(
        _gather_rows_kernel,
        grid_spec=pltpu.PrefetchScalarGridSpec(
            num_scalar_prefetch=1,
            grid=(n_out // GATHER_BLK,),
            in_specs=[pl.BlockSpec(memory_space=pl.ANY)],
            out_specs=pl.BlockSpec((GATHER_BLK, src.shape[1]), lambda b, idx_ref: (b, 0)),
            scratch_shapes=[pltpu.SemaphoreType.DMA(())],
        ),
        out_shape=jax.ShapeDtypeStruct((n_out, src.shape[1]), src.dtype),
        compiler_params=_cparams(("arbitrary",)),
        name=name,
    )(idx, src)


def _moe_up_kernel(blk_e_ref, nact_ref, x_ref, wg_ref, wu_ref, h_ref, wgb, wub):
    b = pl.program_id(1)
    e = blk_e_ref[b]
    prev = blk_e_ref[jnp.maximum(b - 1, 0)]

    @pl.when((b == 0) | (e != prev))
    def _():
        wgb[...] = wg_ref[0].astype(BF16)
        wub[...] = wu_ref[0].astype(BF16)

    @pl.when(b < nact_ref[0])
    def _():
        xa, xb = _unpack_bf16_pair(x_ref[...])
        x = jnp.concatenate([xa.astype(BF16), xb.astype(BF16)], axis=-1)
        g = _dot(x, wgb[...])
        u = _dot(x, wub[...])
        h_ref[...] = (_silu(g) * u).astype(h_ref.dtype)

    @pl.when(b >= nact_ref[0])
    def _():
        h_ref[...] = jnp.zeros_like(h_ref)


def _moe_up(blk_e, nact, xs, w_gate, w_up):
    nht = EXPERT_HIDDEN // MOE_TH
    half = D_MODEL // 2

    def blk(b, nact_ref):
        return jnp.minimum(b, nact_ref[0] - 1)

    return pl.pallas_call(
        _moe_up_kernel,
        grid_spec=pltpu.PrefetchScalarGridSpec(
            num_scalar_prefetch=2,
            grid=(nht, MOE_NBLK),
            in_specs=[
                pl.BlockSpec((MOE_BLK, half), lambda ht, b, be, na: (blk(b, na), 0)),
                pl.BlockSpec((1, D_MODEL, MOE_TH), lambda ht, b, be, na: (be[b], 0, ht)),
                pl.BlockSpec((1, D_MODEL, MOE_TH), lambda ht, b, be, na: (be[b], 0, ht)),
            ],
            out_specs=pl.BlockSpec((MOE_BLK, MOE_TH), lambda ht, b, be, na: (b, ht)),
            scratch_shapes=[pltpu.VMEM((D_MODEL, MOE_TH), BF16), pltpu.VMEM((D_MODEL, MOE_TH), BF16)],
        ),
        out_shape=jax.ShapeDtypeStruct((MOE_ROWS, EXPERT_HIDDEN), BF16),
        compiler_params=_cparams(("arbitrary", "arbitrary")),
        name="moe_up",
    )(blk_e, nact, xs, w_gate, w_up)


def _moe_down_kernel(blk_e_ref, nact_ref, h_ref, wd_ref, y_ref, wdb):
    b = pl.program_id(0)
    e = blk_e_ref[b]
    prev = blk_e_ref[jnp.maximum(b - 1, 0)]

    @pl.when((b == 0) | (e != prev))
    def _():
        wdb[...] = wd_ref[0].astype(BF16)

    @pl.when(b < nact_ref[0])
    def _():
        y = _dot(h_ref[...], wdb[...])
        half = D_MODEL // 2
        y_ref[...] = _pack_bf16_pair(y[:, :half], y[:, half:])

    @pl.when(b >= nact_ref[0])
    def _():
        y_ref[...] = jnp.zeros_like(y_ref)


def _moe_down(blk_e, nact, hdn, w_down):
    half = D_MODEL // 2

    def blk(b, nact_ref):
        return jnp.minimum(b, nact_ref[0] - 1)

    return pl.pallas_call(
        _moe_down_kernel,
        grid_spec=pltpu.PrefetchScalarGridSpec(
            num_scalar_prefetch=2,
            grid=(MOE_NBLK,),
            in_specs=[
                pl.BlockSpec((MOE_BLK, EXPERT_HIDDEN), lambda b, be, na: (blk(b, na), 0)),
                pl.BlockSpec((1, EXPERT_HIDDEN, D_MODEL), lambda b, be, na: (be[b], 0, 0)),
            ],
            out_specs=pl.BlockSpec((MOE_BLK, half), lambda b, be, na: (b, 0)),
            scratch_shapes=[pltpu.VMEM((EXPERT_HIDDEN, D_MODEL), BF16)],
        ),
        out_shape=jax.ShapeDtypeStruct((MOE_ROWS, half), U32),
        compiler_params=_cparams(("arbitrary",)),
        name="moe_down",
    )(blk_e, nact, hdn, w_down)


def _combine_kernel(h_ref, ya0_ref, ya1_ref, route_ref, w_ref, o_ref):
    route = route_ref[...]
    w0 = route[:, 2:3]
    w1 = route[:, 3:4]
    a0, b0 = _unpack_bf16_pair(ya0_ref[...])
    a1, b1 = _unpack_bf16_pair(ya1_ref[...])
    moe = jnp.concatenate([w0 * a0 + w1 * a1, w0 * b0 + w1 * b1], axis=-1)
    h = h_ref[...] + moe
    o_ref[...] = h * lax.rsqrt(jnp.mean(h * h, axis=-1, keepdims=True) + EPS) * w_ref[...]


def _combine(h1, ya, route, norm_w):
    s, d = h1.shape
    nb = s // NORM_ROWS
    return pl.pallas_call(
        _combine_kernel,
        grid=(nb,),
        in_specs=[
            pl.BlockSpec((NORM_ROWS, d), lambda i: (i, 0)),
            pl.BlockSpec((NORM_ROWS, d // 2), lambda i: (i, 0)),
            pl.BlockSpec((NORM_ROWS, d // 2), lambda i: (i + nb, 0)),
            pl.BlockSpec((NORM_ROWS, LANES), lambda i: (i, 0)),
            pl.BlockSpec((1, d), lambda i: (0, 0)),
        ],
        out_specs=pl.BlockSpec((NORM_ROWS, d), lambda i: (i, 0)),
        out_shape=jax.ShapeDtypeStruct((s, d), F32),
        compiler_params=_cparams(("parallel",)),
        name="combine",
    )(h1, ya, ya, route, norm_w.reshape(1, d))


def _dispatch_plan(route):
    s = route.shape[0]
    flat_e = route[:, :EXPERT_TOP_K].astype(I32).reshape(-1)
    onehot = (flat_e[:, None] == jnp.arange(N_EXPERTS, dtype=I32)[None, :]).astype(I32)
    csum = jnp.cumsum(onehot, axis=0)
    rank = jnp.sum((csum - 1) * onehot, axis=1)
    counts = csum[-1]
    padded = (counts + MOE_BLK - 1) // MOE_BLK * MOE_BLK
    pend = jnp.cumsum(padded)
    pstart = pend - padded
    dest = (pstart[flat_e] + rank).astype(I32)
    flat_t = jnp.repeat(jnp.arange(s, dtype=I32), EXPERT_TOP_K)
    row_tok = jnp.zeros((MOE_ROWS,), I32).at[dest].set(flat_t)
    nact = (pend[-1] // MOE_BLK).astype(I32).reshape(1)
    blk_start = jnp.arange(MOE_NBLK, dtype=I32) * MOE_BLK
    blk_e = jnp.minimum(jnp.searchsorted(pend, blk_start, side='right'), N_EXPERTS - 1).astype(I32)
    last_e = blk_e[jnp.maximum(nact[0] - 1, 0)]
    blk_e = jnp.where(jnp.arange(MOE_NBLK) < nact[0], blk_e, last_e)
    return dest, row_tok, blk_e, nact


def kernel(x, norm_mix_w, w_in, conv_w, conv_b, dt_bias, a_log, d_skip, ssd_norm_w, attn_norm_w, w_out,
           norm_ffn_w, w_group, b_group, w_expert, b_expert, w_gate, w_up, w_down, norm_final_w):
    b, s, d = x.shape
    x2 = x.reshape(b * s, d)
    n_qkv = 3 * ATT_WIDTH
    n_zxbc = SSM_WIDTH + SSM_WIDTH + 2 * SSM_GROUPS * SSM_STATE

    w_qkv = w_in[:, :n_qkv].astype(BF16)
    w_zxbc = w_in[:, n_qkv:n_qkv + n_zxbc].astype(BF16)
    w_dt = jnp.pad(w_in[:, n_qkv + n_zxbc:], ((0, 0), (0, LANES - SSM_HEADS))).astype(BF16)
    w_out_b = w_out.astype(BF16)
    wr = jnp.concatenate(
        [w_group, jnp.transpose(w_expert, (1, 0, 2)).reshape(d, N_EXPERTS),
         jnp.zeros((d, LANES - MOE_GROUPS - N_EXPERTS), F32)], axis=1)
    wr_hi = wr.astype(BF16)
    wr_lo = (wr - wr_hi.astype(F32)).astype(BF16)
    br = jnp.concatenate([b_group, b_expert.reshape(-1), jnp.zeros((LANES - MOE_GROUPS - N_EXPERTS,), F32)]).reshape(1, LANES)
    kk = jnp.arange(ATT_BLK)
    uo = jnp.concatenate([(kk[:, None] > kk[None, :]).astype(BF16), jnp.ones((ATT_BLK, ATT_BLK), BF16)], axis=1)

    u, dt_raw = _norm_dt(x2, norm_mix_w, w_dt)
    qkv = _proj_heads(u, w_qkv)
    zxbc = _proj(u, w_zxbc, F32, "proj_zxbc")
    att = _attention(qkv, attn_norm_w, uo)
    y = _ssd(zxbc, dt_raw, conv_w, conv_b, dt_bias, a_log, d_skip, ssd_norm_w)
    h1 = _out_proj(att, y, w_out_b, x2)

    up, route = _router(h1, norm_ffn_w, wr_hi, wr_lo, br)
    dest, row_tok, blk_e, nact = _dispatch_plan(route)
    xs = _gather_rows(row_tok, up, MOE_ROWS, "dispatch")
    hdn = _moe_up(blk_e, nact, xs, w_gate, w_up)
    ys = _moe_down(blk_e, nact, hdn, w_down)
    slot_major = dest.reshape(s, EXPERT_TOP_K).T.reshape(-1)
    ya = _gather_rows(slot_major, ys, s * EXPERT_TOP_K, "undispatch")
    out = _combine(h1, ya, route, norm_final_w)
    return out.reshape(b, s, d)
```

```python
import functools
import math

import jax
import jax.numpy as jnp
from jax import lax
from jax.experimental import pallas as pl
from jax.experimental.pallas import tpu as pltpu

F32 = jnp.float32
BF16 = jnp.bfloat16
U32 = jnp.uint32
I32 = jnp.int32

D_MODEL = 4096
SEQ = 8192
ATT_WIDTH = 2048
ATT_HEAD_DIM = 128
ATT_HEADS = 16
SSM_WIDTH = 2048
SSM_HEAD_DIM = 64
SSM_HEADS = 32
SSM_STATE = 128
SSM_GROUPS = 8
SSM_GROUP_WIDTH = SSM_WIDTH // SSM_GROUPS
CONV_WIDTH = 4
MOE_GROUPS = 8
EXPERTS_PER_GROUP = 8
N_EXPERTS = 64
EXPERT_TOP_K = 2
EXPERT_HIDDEN = 768
EPS = 1e-6

LANES = 128
SUBLANES = 8
VMEM_LIMIT = 56 * 1024 * 1024

NORM_ROWS = 256
MM_TM = 1024
MM_TN = 1024
ATT_BLK = 128
ATT_HPS = 4
SSD_T = 128
MOE_BLK = 128
MOE_TH = 384
MOE_NBLK = -(-(SEQ * EXPERT_TOP_K + N_EXPERTS * (MOE_BLK - 1)) // MOE_BLK)
MOE_ROWS = MOE_NBLK * MOE_BLK
GATHER_BLK = 512
GATHER_UNROLL = 8
ATT_UNDERFLOW = 104.5

NEG_BIG = -1e30


def _cparams(sem):
    return pltpu.CompilerParams(dimension_semantics=sem, vmem_limit_bytes=VMEM_LIMIT)


def _softplus(x):
    return jnp.maximum(x, 0.0) + jnp.log(1.0 + jnp.exp(-jnp.abs(x)))


def _silu(x):
    return x / (1.0 + jnp.exp(-x))


def _split2(x):
    hi = x.astype(BF16)
    lo = (x - hi.astype(F32)).astype(BF16)
    return hi, lo


def _split3(x):
    h1 = x.astype(BF16)
    r1 = x - h1.astype(F32)
    h2 = r1.astype(BF16)
    h3 = (r1 - h2.astype(F32)).astype(BF16)
    return h1, h2, h3


def _dot(a, b):
    return jnp.dot(a, b, preferred_element_type=F32)


def _dot_nt(a, b):
    return lax.dot_general(a, b, (((1,), (1,)), ((), ())), preferred_element_type=F32)


def _dot_exact01(x, m, parts):
    terms = _split3(x) if parts == 3 else _split2(x)
    acc = _dot(terms[0], m)
    for t in terms[1:]:
        acc = acc + _dot(t, m)
    return acc


def _pack_bf16_pair(a, b):
    ab = pltpu.bitcast(a.astype(BF16).astype(F32), U32)
    bb = pltpu.bitcast(b.astype(BF16).astype(F32), U32)
    return ab | (bb >> 16)


def _unpack_bf16_pair(p):
    a = pltpu.bitcast(p & jnp.uint32(0xFFFF0000), F32)
    b = pltpu.bitcast(p << 16, F32)
    return a, b


def _norm_dt_kernel(x_ref, w_ref, wdt_ref, u_ref, dt_ref):
    x = x_ref[...]
    y = x * lax.rsqrt(jnp.mean(x * x, axis=-1, keepdims=True) + EPS) * w_ref[...]
    ub = y.astype(BF16)
    u_ref[...] = ub
    dt_ref[...] = _dot(ub, wdt_ref[...])


def _norm_dt(x2, norm_w, wdt_b):
    s, d = x2.shape
    return pl.pallas_call(
        _norm_dt_kernel,
        grid=(s // NORM_ROWS,),
        in_specs=[
            pl.BlockSpec((NORM_ROWS, d), lambda i: (i, 0)),
            pl.BlockSpec((1, d), lambda i: (0, 0)),
            pl.BlockSpec((d, LANES), lambda i: (0, 0)),
        ],
        out_specs=[
            pl.BlockSpec((NORM_ROWS, d), lambda i: (i, 0)),
            pl.BlockSpec((NORM_ROWS, LANES), lambda i: (i, 0)),
        ],
        out_shape=[jax.ShapeDtypeStruct((s, d), BF16), jax.ShapeDtypeStruct((s, LANES), F32)],
        compiler_params=_cparams(("parallel",)),
        name="norm_dt",
    )(x2, norm_w.reshape(1, d), wdt_b)


def _proj_heads_kernel(u_ref, w_ref, o_ref):
    acc = _dot(u_ref[...], w_ref[...])
    for hh in range(MM_TN // ATT_HEAD_DIM):
        o_ref[hh] = acc[:, hh * ATT_HEAD_DIM:(hh + 1) * ATT_HEAD_DIM].astype(o_ref.dtype)


def _proj_heads(u, w_b):
    s, d = u.shape
    n = w_b.shape[1]
    hpt = MM_TN // ATT_HEAD_DIM
    return pl.pallas_call(
        _proj_heads_kernel,
        grid=(n // MM_TN, s // MM_TM),
        in_specs=[
            pl.BlockSpec((MM_TM, d), lambda j, i: (i, 0)),
            pl.BlockSpec((d, MM_TN), lambda j, i: (0, j)),
        ],
        out_specs=pl.BlockSpec((hpt, MM_TM, ATT_HEAD_DIM), lambda j, i: (j, i, 0)),
        out_shape=jax.ShapeDtypeStruct((n // ATT_HEAD_DIM, s, ATT_HEAD_DIM), BF16),
        compiler_params=_cparams(("parallel", "parallel")),
        name="proj_qkv",
    )(u, w_b)


def _proj_kernel(u_ref, w_ref, o_ref):
    o_ref[...] = _dot(u_ref[...], w_ref[...]).astype(o_ref.dtype)


def _proj(u, w_b, out_dtype, name):
    s, d = u.shape
    n = w_b.shape[1]
    return pl.pallas_call(
        _proj_kernel,
        grid=(n // MM_TN, s // MM_TM),
        in_specs=[
            pl.BlockSpec((MM_TM, d), lambda j, i: (i, 0)),
            pl.BlockSpec((d, MM_TN), lambda j, i: (0, j)),
        ],
        out_specs=pl.BlockSpec((MM_TM, MM_TN), lambda j, i: (i, j)),
        out_shape=jax.ShapeDtypeStruct((s, n), out_dtype),
        compiler_params=_cparams(("parallel", "parallel")),
        name=name,
    )(u, w_b)


def _attn_kernel(q_ref, k_ref, v_ref, nw_ref, uo_ref, o_ref, *scratch):
    acc_refs, c_refs = scratch[:ATT_HPS], scratch[ATT_HPS:]
    _attn_body(q_ref, k_ref, v_ref, nw_ref, uo_ref, o_ref, acc_refs, c_refs)


def _attn_body(q_ref, k_ref, v_ref, nw_ref, uo_ref, o_ref, acc_refs, c_refs):
    i = pl.program_id(1)
    uo = uo_ref[...]
    scale = 1.0 / math.sqrt(ATT_HEAD_DIM)
    row = lax.broadcasted_iota(I32, (ATT_BLK, ATT_BLK), 0)
    col = lax.broadcasted_iota(I32, (ATT_BLK, ATT_BLK), 1)
    strict = col < row

    def block(j, masked):
        off = pl.multiple_of(j * ATT_BLK, ATT_BLK)
        heads = range(ATT_HPS)
        zs = [_dot_nt(q_ref[hh], k_ref[hh, pl.ds(off, ATT_BLK), :]) * scale for hh in heads]
        sps = [_softplus(z) for z in zs]
        log_betas = [z - sp for z, sp in zip(zs, sps)]
        log_1ms = [jnp.where(strict, -sp, 0.0) if masked else -sp for sp in sps]
        splits = [_split2(x) for x in log_1ms]
        r2s = [_dot(hi, uo) + _dot(lo, uo) for hi, lo in splits]
        cmax = None
        ws = []
        for hh in heads:
            r = r2s[hh][:, :ATT_BLK]
            tot = r2s[hh][:, ATT_BLK:]
            if masked:
                ws.append(jnp.where(strict, jnp.exp(log_betas[hh] + r), 0.0).astype(BF16))
                c_new = tot
            else:
                c_old = c_refs[hh][...]
                ws.append(jnp.exp(log_betas[hh] + r + c_old).astype(BF16))
                c_new = c_old + tot
            c_refs[hh][...] = c_new
            cmax = c_new if cmax is None else jnp.maximum(cmax, c_new)
        for hh in heads:
            pv = _dot(ws[hh], v_ref[hh, pl.ds(off, ATT_BLK), :])
            if masked:
                acc_refs[hh][...] = pv
            else:
                acc_refs[hh][...] += pv
        return jnp.max(cmax)

    def cond(carry):
        return (carry[0] < i) & (carry[1] > -ATT_UNDERFLOW)

    def body(carry):
        t = carry[0]
        return t + 1, block(i - 1 - t, False)

    lax.while_loop(cond, body, (jnp.int32(0), block(i, True)))
    for hh in range(ATT_HPS):
        acc = acc_refs[hh][...]
        y = acc * lax.rsqrt(jnp.mean(acc * acc, axis=-1, keepdims=True) + EPS) * nw_ref[hh]
        o_ref[:, hh * ATT_HEAD_DIM:(hh + 1) * ATT_HEAD_DIM] = y.astype(o_ref.dtype)


def _attention(qkv, attn_norm_w, uo):
    s = qkv.shape[1]
    ng = ATT_HEADS // ATT_HPS
    return pl.pallas_call(
        _attn_kernel,
        grid=(ng, s // ATT_BLK),
        in_specs=[
            pl.BlockSpec((ATT_HPS, ATT_BLK, ATT_HEAD_DIM), lambda g, i: (g, i, 0)),
            pl.BlockSpec((ATT_HPS, s, ATT_HEAD_DIM), lambda g, i: (ng + g, 0, 0)),
            pl.BlockSpec((ATT_HPS, s, ATT_HEAD_DIM), lambda g, i: (2 * ng + g, 0, 0)),
            pl.BlockSpec((ATT_HPS, 1, ATT_HEAD_DIM), lambda g, i: (g, 0, 0)),
            pl.BlockSpec((ATT_BLK, 2 * ATT_BLK), lambda g, i: (0, 0)),
        ],
        out_specs=pl.BlockSpec((ATT_BLK, ATT_HPS * ATT_HEAD_DIM), lambda g, i: (i, g)),
        out_shape=jax.ShapeDtypeStruct((s, ATT_WIDTH), BF16),
        scratch_shapes=(
            [pltpu.VMEM((ATT_BLK, ATT_HEAD_DIM), F32) for _ in range(ATT_HPS)]
            + [pltpu.VMEM((ATT_BLK, ATT_BLK), F32) for _ in range(ATT_HPS)]
        ),
        compiler_params=_cparams(("parallel", "arbitrary")),
        name="sb_attn",
    )(qkv, qkv, qkv, attn_norm_w.reshape(ATT_HEADS, 1, ATT_HEAD_DIM), uo)


def _ssd_kernel(z_ref, xs_ref, bc_ref, dt_ref, cw_ref, cb_ref, dtb_ref, alog_ref, dskip_ref, nw_ref,
                tri_ref, eh_ref, et_ref, o_ref,
                xbuf, bbuf, xc, bcc, acol, acst, dtf, decf, expf, state):
    c = pl.program_id(0)
    t = SSD_T
    w2 = SSM_WIDTH

    @pl.when(c == 0)
    def _():
        xbuf[0:SUBLANES, :] = jnp.zeros((SUBLANES, w2), F32)
        bbuf[0:SUBLANES, :] = jnp.zeros((SUBLANES, w2), F32)
        state[...] = jnp.zeros_like(state)

    xbuf[SUBLANES:SUBLANES + t, :] = xs_ref[...]
    bbuf[SUBLANES:SUBLANES + t, :] = bc_ref[...]
    cw = cw_ref[...]
    cbias = cb_ref[...]
    accx = jnp.zeros((t, w2), F32) + cbias[:, :w2]
    accb = jnp.zeros((t, w2), F32) + cbias[:, w2:]
    for k in range(CONV_WIDTH):
        start = SUBLANES - (CONV_WIDTH - 1) + k
        accx = accx + xbuf[start:start + t, :] * cw[k:k + 1, :w2]
        accb = accb + bbuf[start:start + t, :] * cw[k:k + 1, w2:]
    xc[...] = _silu(accx)
    bcc[...] = _silu(accb)
    xbuf[0:SUBLANES, :] = xs_ref[t - SUBLANES:t, :]
    bbuf[0:SUBLANES, :] = bc_ref[t - SUBLANES:t, :]

    dt = _softplus(dt_ref[...] + dtb_ref[...])
    a = -jnp.exp(alog_ref[...])
    a_cs = _dot_exact01_lhs(tri_ref[...], dt * a)
    a_last = a_cs[t - 1:t, :]
    eh = eh_ref[...]
    dtf[...] = _dot_exact01(dt, eh, 2)
    decf[...] = _dot_exact01(dt * jnp.exp(a_last - a_cs), eh, 2)
    expf[...] = _dot_exact01(jnp.exp(a_cs), eh, 2)
    acol[...] = _dot_exact01(a_cs, et_ref[...], 3)
    acst[...] = a_cs.T

    row = lax.broadcasted_iota(I32, (t, t), 0)
    col = lax.broadcasted_iota(I32, (t, t), 1)
    lower = col <= row
    head_of_lane = lax.broadcasted_iota(I32, (t, SSM_GROUP_WIDTH), 1) // SSM_HEAD_DIM

    def group(g, carry):
        go = pl.multiple_of(g * SSM_GROUP_WIDTH, SSM_GROUP_WIDTH)
        no = pl.multiple_of(g * SSM_STATE, SSM_STATE)
        bg = bcc[:, pl.ds(no, SSM_STATE)]
        cg = bcc[:, pl.ds(SSM_GROUPS * SSM_STATE + no, SSM_STATE)]
        cgb = cg.astype(BF16)
        cb = _dot_nt(cgb, bg.astype(BF16))
        xg = xc[:, pl.ds(go, SSM_GROUP_WIDTH)]
        xd = xg * dtf[:, pl.ds(go, SSM_GROUP_WIDTH)]
        xdec = (xg * decf[:, pl.ds(go, SSM_GROUP_WIDTH)]).astype(BF16)
        st = state[g]
        y = _dot(cgb, st.astype(BF16)) * expf[:, pl.ds(go, SSM_GROUP_WIDTH)]
        for r in range(SSM_GROUP_WIDTH // SSM_HEAD_DIM):
            h = g * (SSM_GROUP_WIDTH // SSM_HEAD_DIM) + r
            ho = pl.multiple_of(h * t, t)
            seg = acol[:, pl.ds(ho, t)] - acst[pl.ds(h, 1), :]
            lmat = jnp.exp(jnp.where(lower, seg, NEG_BIG))
            gm = (cb * lmat).astype(BF16)
            xd_r = jnp.where(head_of_lane == r, xd, 0.0).astype(BF16)
            y = y + _dot(gm, xd_r)
        state[g] = st * expf[t - 1:t, pl.ds(go, SSM_GROUP_WIDTH)] + _dot(bg.T.astype(BF16), xdec)
        y = y + dskip_ref[:, pl.ds(go, SSM_GROUP_WIDTH)] * xg
        y = y * _silu(z_ref[:, pl.ds(go, SSM_GROUP_WIDTH)])
        y = y * lax.rsqrt(jnp.mean(y * y, axis=-1, keepdims=True) + EPS) * nw_ref[:, pl.ds(go, SSM_GROUP_WIDTH)]
        o_ref[:, pl.ds(go, SSM_GROUP_WIDTH)] = y.astype(o_ref.dtype)
        return carry

    lax.fori_loop(0, SSM_GROUPS, group, 0)


def _dot_exact01_lhs(m, x):
    h1, h2, h3 = _split3(x)
    return _dot(m, h1) + _dot(m, h2) + _dot(m, h3)


def _ssd(zxbc, dt_raw, conv_w, conv_b, dt_bias, a_log, d_skip, ssd_norm_w):
    s = zxbc.shape[0]
    t = SSD_T
    w2 = SSM_WIDTH
    pad = LANES - SSM_HEADS
    tri = (jnp.arange(t)[:, None] >= jnp.arange(t)[None, :]).astype(BF16)
    heads = jnp.arange(LANES)[:, None]
    eh = (heads == (jnp.arange(w2)[None, :] // SSM_HEAD_DIM)).astype(BF16)
    et = (heads == (jnp.arange(SSM_HEADS * t)[None, :] // t)).astype(BF16)
    dtb = jnp.pad(dt_bias.astype(F32), (0, pad)).reshape(1, LANES)
    alog = jnp.pad(a_log.astype(F32), (0, pad)).reshape(1, LANES)
    dskip = jnp.repeat(d_skip.astype(F32), SSM_HEAD_DIM).reshape(1, w2)
    const = lambda c: (0, 0)
    return pl.pallas_call(
        _ssd_kernel,
        grid=(s // t,),
        in_specs=[
            pl.BlockSpec((t, w2), lambda c: (c, 0)),
            pl.BlockSpec((t, w2), lambda c: (c, 1)),
            pl.BlockSpec((t, w2), lambda c: (c, 2)),
            pl.BlockSpec((t, LANES), lambda c: (c, 0)),
            pl.BlockSpec((CONV_WIDTH, 2 * w2), const),
            pl.BlockSpec((1, 2 * w2), const),
            pl.BlockSpec((1, LANES), const),
            pl.BlockSpec((1, LANES), const),
            pl.BlockSpec((1, w2), const),
            pl.BlockSpec((1, w2), const),
            pl.BlockSpec((t, t), const),
            pl.BlockSpec((LANES, w2), const),
            pl.BlockSpec((LANES, SSM_HEADS * t), const),
        ],
        out_specs=pl.BlockSpec((t, w2), lambda c: (c, 0)),
        out_shape=jax.ShapeDtypeStruct((s, w2), BF16),
        scratch_shapes=[
            pltpu.VMEM((t + SUBLANES, w2), F32),
            pltpu.VMEM((t + SUBLANES, w2), F32),
            pltpu.VMEM((t, w2), F32),
            pltpu.VMEM((t, w2), F32),
            pltpu.VMEM((t, SSM_HEADS * t), F32),
            pltpu.VMEM((LANES, t), F32),
            pltpu.VMEM((t, w2), F32),
            pltpu.VMEM((t, w2), F32),
            pltpu.VMEM((t, w2), F32),
            pltpu.VMEM((SSM_GROUPS, SSM_STATE, SSM_GROUP_WIDTH), F32),
        ],
        compiler_params=_cparams(("arbitrary",)),
        name="ssd",
    )(zxbc, zxbc, zxbc, dt_raw, conv_w.astype(F32), conv_b.astype(F32).reshape(1, 2 * w2), dtb, alog, dskip,
      ssd_norm_w.astype(F32).reshape(1, w2), tri, eh, et)


def _out_proj_kernel(att_ref, y_ref, w_ref, x_ref, o_ref):
    acc = _dot(att_ref[...], w_ref[0:ATT_WIDTH, :])
    acc = acc + _dot(y_ref[...], w_ref[ATT_WIDTH:, :])
    o_ref[...] = x_ref[...] + acc


def _out_proj(att, y, w_b, x2):
    s, d = x2.shape
    tm, tn = 512, MM_TN
    return pl.pallas_call(
        _out_proj_kernel,
        grid=(d // tn, s // tm),
        in_specs=[
            pl.BlockSpec((tm, ATT_WIDTH), lambda j, i: (i, 0)),
            pl.BlockSpec((tm, SSM_WIDTH), lambda j, i: (i, 0)),
            pl.BlockSpec((ATT_WIDTH + SSM_WIDTH, tn), lambda j, i: (0, j)),
            pl.BlockSpec((tm, tn), lambda j, i: (i, j)),
        ],
        out_specs=pl.BlockSpec((tm, tn), lambda j, i: (i, j)),
        out_shape=jax.ShapeDtypeStruct((s, d), F32),
        compiler_params=_cparams(("parallel", "parallel")),
        name="out_proj",
    )(att, y, w_b, x2)


def _router_kernel(h_ref, w_ref, whi_ref, wlo_ref, b_ref, up_ref, route_ref):
    x = h_ref[...]
    u = x * lax.rsqrt(jnp.mean(x * x, axis=-1, keepdims=True) + EPS) * w_ref[...]
    half = D_MODEL // 2
    up_ref[...] = _pack_bf16_pair(u[:, :half], u[:, half:])
    uh, ul = _split2(u)
    whi = whi_ref[...]
    logits = _dot(uh, whi) + _dot(ul, whi) + _dot(uh, wlo_ref[...]) + b_ref[...]
    lane = lax.broadcasted_iota(I32, logits.shape, 1).astype(F32)
    ninf = -jnp.inf
    nolane = float(LANES)
    gl = jnp.where(lane < MOE_GROUPS, logits, ninf)
    gmax = jnp.max(gl, axis=-1, keepdims=True)
    gsum = jnp.sum(jnp.exp(gl - gmax), axis=-1, keepdims=True)
    g_w = 1.0 / gsum
    g_idx = jnp.min(jnp.where(gl == gmax, lane, nolane), axis=-1, keepdims=True)
    lo_lane = MOE_GROUPS + g_idx * EXPERTS_PER_GROUP
    el = jnp.where((lane >= lo_lane) & (lane < lo_lane + EXPERTS_PER_GROUP), logits, ninf)
    m0 = jnp.max(el, axis=-1, keepdims=True)
    i0 = jnp.min(jnp.where(el == m0, lane, nolane), axis=-1, keepdims=True)
    el2 = jnp.where(lane == i0, ninf, el)
    m1 = jnp.max(el2, axis=-1, keepdims=True)
    i1 = jnp.min(jnp.where(el2 == m1, lane, nolane), axis=-1, keepdims=True)
    tt = jnp.exp(m1 - m0)
    w0 = g_w / (1.0 + tt)
    w1 = g_w * tt / (1.0 + tt)
    e0 = i0 - MOE_GROUPS
    e1 = i1 - MOE_GROUPS
    out = jnp.where(lane == 0, e0, jnp.where(lane == 1, e1, jnp.where(lane == 2, w0, jnp.where(lane == 3, w1, 0.0))))
    route_ref[...] = out


def _router(h1, norm_w, wr_hi, wr_lo, br):
    s, d = h1.shape
    return pl.pallas_call(
        _router_kernel,
        grid=(s // NORM_ROWS,),
        in_specs=[
            pl.BlockSpec((NORM_ROWS, d), lambda i: (i, 0)),
            pl.BlockSpec((1, d), lambda i: (0, 0)),
            pl.BlockSpec((d, LANES), lambda i: (0, 0)),
            pl.BlockSpec((d, LANES), lambda i: (0, 0)),
            pl.BlockSpec((1, LANES), lambda i: (0, 0)),
        ],
        out_specs=[
            pl.BlockSpec((NORM_ROWS, d // 2), lambda i: (i, 0)),
            pl.BlockSpec((NORM_ROWS, LANES), lambda i: (i, 0)),
        ],
        out_shape=[jax.ShapeDtypeStruct((s, d // 2), U32), jax.ShapeDtypeStruct((s, LANES), F32)],
        compiler_params=_cparams(("parallel",)),
        name="router",
    )(h1, norm_w.reshape(1, d), wr_hi, wr_lo, br)


def _gather_rows_kernel(idx_ref, nvalid_ref, src_ref, o_ref, sem):
    base = pl.program_id(0) * GATHER_BLK

    def copy(r):
        return pltpu.make_async_copy(src_ref.at[pl.ds(idx_ref[base + r], 1)], o_ref.at[pl.ds(r, 1)], sem)

    @pl.when(base < nvalid_ref[0])
    def _():
        def issue(r, carry):
            copy(r).start()
            return carry

        lax.fori_loop(0, GATHER_BLK, issue, 0, unroll=GATHER_UNROLL)

        def drain(r, carry):
            copy(r).wait()
            return carry

        lax.fori_loop(0, GATHER_BLK, drain, 0, unroll=GATHER_UNROLL)

    @pl.when(base >= nvalid_ref[0])
    def _():
        o_ref[...] = jnp.zeros_like(o_ref)


def _gather_rows(idx, nvalid, src, n_out, name):
    return pl.pallas_call(
        _gather_rows_kernel,
        grid_spec=pltpu.PrefetchScalarGridSpec(
            num_scalar_prefetch=2,
            grid=(n_out // GATHER_BLK,),
            in_specs=[pl.BlockSpec(memory_space=pl.ANY)],
            out_specs=pl.BlockSpec((GATHER_BLK, src.shape[1]), lambda b, idx_ref, nv_ref: (b, 0)),
            scratch_shapes=[pltpu.SemaphoreType.DMA(())],
        ),
        out_shape=jax.ShapeDtypeStruct((n_out, src.shape[1]), src.dtype),
        compiler_params=_cparams(("arbitrary",)),
        name=name,
    )(idx, nvalid, src)


def _zero_fill_blocks(zbuf, dst_hbm, first_blk, sem):
    zbuf[...] = jnp.zeros_like(zbuf)

    def fill(b, carry):
        r0 = pl.multiple_of(b * MOE_BLK, MOE_BLK)
        cp = pltpu.make_async_copy(zbuf, dst_hbm.at[pl.ds(r0, MOE_BLK)], sem)
        cp.start()
        cp.wait()
        return carry

    lax.fori_loop(first_blk, MOE_NBLK, fill, 0)


def _walk_blocks(n, in_copy, out_copy, compute):
    @pl.when(n > 0)
    def _():
        in_copy(0, 0).start()

    def body(j, carry):
        slot = j % 2
        in_copy(j, slot).wait()

        @pl.when(j + 1 < n)
        def _():
            in_copy(j + 1, 1 - slot).start()

        @pl.when(j >= 2)
        def _():
            out_copy(j - 2, slot).wait()

        compute(slot)
        out_copy(j, slot).start()
        return carry

    lax.fori_loop(0, n, body, 0)

    @pl.when(n >= 2)
    def _():
        out_copy(n - 2, n % 2).wait()

    @pl.when(n >= 1)
    def _():
        out_copy(n - 1, (n + 1) % 2).wait()


def _moe_up_kernel(first_ref, cnt_ref, nact_ref, xs_hbm, wg_ref, wu_ref, hdn_hbm,
                   wgb, wub, xbuf, hbuf, zbuf, in_sem, out_sem, z_sem):
    e = pl.program_id(0)
    ht = pl.program_id(1)
    n = cnt_ref[e]
    b0 = first_ref[e]
    col0 = pl.multiple_of(ht * MOE_TH, LANES)

    def rows(j):
        return pl.ds(pl.multiple_of((b0 + j) * MOE_BLK, MOE_BLK), MOE_BLK)

    def in_copy(j, slot):
        return pltpu.make_async_copy(xs_hbm.at[rows(j)], xbuf.at[slot], in_sem.at[slot])

    def out_copy(j, slot):
        return pltpu.make_async_copy(hbuf.at[slot], hdn_hbm.at[rows(j), pl.ds(col0, MOE_TH)], out_sem.at[slot])

    @pl.when((e == 0) & (ht == 0))
    def _():
        _zero_fill_blocks(zbuf, hdn_hbm, nact_ref[0], z_sem)

    @pl.when(n > 0)
    def _():
        wgb[...] = wg_ref[0].astype(BF16)
        wub[...] = wu_ref[0].astype(BF16)

    def compute(slot):
        xa, xb = _unpack_bf16_pair(xbuf[slot])
        x = jnp.concatenate([xa.astype(BF16), xb.astype(BF16)], axis=-1)
        g = _dot(x, wgb[...])
        u = _dot(x, wub[...])
        hbuf[slot] = (_silu(g) * u).astype(BF16)

    _walk_blocks(n, in_copy, out_copy, compute)


def _moe_up(blk_first, blk_cnt, nact, xs, w_gate, w_up):
    nht = EXPERT_HIDDEN // MOE_TH
    half = D_MODEL // 2
    return pl.pallas_call(
        _moe_up_kernel,
        grid_spec=pltpu.PrefetchScalarGridSpec(
            num_scalar_prefetch=3,
            grid=(N_EXPERTS, nht),
            in_specs=[
                pl.BlockSpec(memory_space=pl.ANY),
                pl.BlockSpec((1, D_MODEL, MOE_TH), lambda e, ht, bf, bc, na: (e, 0, ht)),
                pl.BlockSpec((1, D_MODEL, MOE_TH), lambda e, ht, bf, bc, na: (e, 0, ht)),
            ],
            out_specs=pl.BlockSpec(memory_space=pl.ANY),
            scratch_shapes=[
                pltpu.VMEM((D_MODEL, MOE_TH), BF16),
                pltpu.VMEM((D_MODEL, MOE_TH), BF16),
                pltpu.VMEM((2, MOE_BLK, half), U32),
                pltpu.VMEM((2, MOE_BLK, MOE_TH), BF16),
                pltpu.VMEM((MOE_BLK, EXPERT_HIDDEN), BF16),
                pltpu.SemaphoreType.DMA((2,)),
                pltpu.SemaphoreType.DMA((2,)),
                pltpu.SemaphoreType.DMA(()),
            ],
        ),
        out_shape=jax.ShapeDtypeStruct((MOE_ROWS, EXPERT_HIDDEN), BF16),
        compiler_params=_cparams(("arbitrary", "arbitrary")),
        name="moe_up",
    )(blk_first, blk_cnt, nact, xs, w_gate, w_up)


def _moe_down_kernel(first_ref, cnt_ref, nact_ref, hdn_hbm, wd_ref, ys_hbm,
                     wdb, hbuf, ybuf, zbuf, in_sem, out_sem, z_sem):
    e = pl.program_id(0)
    n = cnt_ref[e]
    b0 = first_ref[e]
    half = D_MODEL // 2

    def rows(j):
        return pl.ds(pl.multiple_of((b0 + j) * MOE_BLK, MOE_BLK), MOE_BLK)

    def in_copy(j, slot):
        return pltpu.make_async_copy(hdn_hbm.at[rows(j)], hbuf.at[slot], in_sem.at[slot])

    def out_copy(j, slot):
        return pltpu.make_async_copy(ybuf.at[slot], ys_hbm.at[rows(j)], out_sem.at[slot])

    @pl.when(e == 0)
    def _():
        _zero_fill_blocks(zbuf, ys_hbm, nact_ref[0], z_sem)

    @pl.when(n > 0)
    def _():
        wdb[...] = wd_ref[0].astype(BF16)

    def compute(slot):
        y = _dot(hbuf[slot], wdb[...])
        ybuf[slot] = _pack_bf16_pair(y[:, :half], y[:, half:])

    _walk_blocks(n, in_copy, out_copy, compute)


def _moe_down(blk_first, blk_cnt, nact, hdn, w_down):
    half = D_MODEL // 2
    return pl.pallas_call(
        _moe_down_kernel,
        grid_spec=pltpu.PrefetchScalarGridSpec(
            num_scalar_prefetch=3,
            grid=(N_EXPERTS,),
            in_specs=[
                pl.BlockSpec(memory_space=pl.ANY),
                pl.BlockSpec((1, EXPERT_HIDDEN, D_MODEL), lambda e, bf, bc, na: (e, 0, 0)),
            ],
            out_specs=pl.BlockSpec(memory_space=pl.ANY),
            scratch_shapes=[
                pltpu.VMEM((EXPERT_HIDDEN, D_MODEL), BF16),
                pltpu.VMEM((2, MOE_BLK, EXPERT_HIDDEN), BF16),
                pltpu.VMEM((2, MOE_BLK, half), U32),
                pltpu.VMEM((MOE_BLK, half), U32),
                pltpu.SemaphoreType.DMA((2,)),
                pltpu.SemaphoreType.DMA((2,)),
                pltpu.SemaphoreType.DMA(()),
            ],
        ),
        out_shape=jax.ShapeDtypeStruct((MOE_ROWS, half), U32),
        compiler_params=_cparams(("arbitrary",)),
        name="moe_down",
    )(blk_first, blk_cnt, nact, hdn, w_down)


def _combine_kernel(h_ref, ya0_ref, ya1_ref, route_ref, w_ref, o_ref):
    route = route_ref[...]
    w0 = route[:, 2:3]
    w1 = route[:, 3:4]
    a0, b0 = _unpack_bf16_pair(ya0_ref[...])
    a1, b1 = _unpack_bf16_pair(ya1_ref[...])
    moe = jnp.concatenate([w0 * a0 + w1 * a1, w0 * b0 + w1 * b1], axis=-1)
    h = h_ref[...] + moe
    o_ref[...] = h * lax.rsqrt(jnp.mean(h * h, axis=-1, keepdims=True) + EPS) * w_ref[...]


def _combine(h1, ya, route, norm_w):
    s, d = h1.shape
    nb = s // NORM_ROWS
    return pl.pallas_call(
        _combine_kernel,
        grid=(nb,),
        in_specs=[
            pl.BlockSpec((NORM_ROWS, d), lambda i: (i, 0)),
            pl.BlockSpec((NORM_ROWS, d // 2), lambda i: (i, 0)),
            pl.BlockSpec((NORM_ROWS, d // 2), lambda i: (i + nb, 0)),
            pl.BlockSpec((NORM_ROWS, LANES), lambda i: (i, 0)),
            pl.BlockSpec((1, d), lambda i: (0, 0)),
        ],
        out_specs=pl.BlockSpec((NORM_ROWS, d), lambda i: (i, 0)),
        out_shape=jax.ShapeDtypeStruct((s, d), F32),
        compiler_params=_cparams(("parallel",)),
        name="combine",
    )(h1, ya, ya, route, norm_w.reshape(1, d))


def _dispatch_plan(route):
    s = route.shape[0]
    flat_e = route[:, :EXPERT_TOP_K].astype(I32).reshape(-1)
    onehot = (flat_e[:, None] == jnp.arange(N_EXPERTS, dtype=I32)[None, :]).astype(I32)
    csum = jnp.cumsum(onehot, axis=0)
    rank = jnp.sum((csum - 1) * onehot, axis=1)
    counts = csum[-1]
    padded = (counts + MOE_BLK - 1) // MOE_BLK * MOE_BLK
    pend = jnp.cumsum(padded)
    pstart = pend - padded
    dest = (pstart[flat_e] + rank).astype(I32)
    flat_t = jnp.repeat(jnp.arange(s, dtype=I32), EXPERT_TOP_K)
    row_tok = (jnp.arange(MOE_ROWS, dtype=I32) % s).at[dest].set(flat_t)
    nact = (pend[-1] // MOE_BLK).astype(I32).reshape(1)
    blk_first = (pstart // MOE_BLK).astype(I32)
    blk_cnt = (padded // MOE_BLK).astype(I32)
    return dest, row_tok, blk_first, blk_cnt, nact


def kernel(x, norm_mix_w, w_in, conv_w, conv_b, dt_bias, a_log, d_skip, ssd_norm_w, attn_norm_w, w_out,
           norm_ffn_w, w_group, b_group, w_expert, b_expert, w_gate, w_up, w_down, norm_final_w):
    b, s, d = x.shape
    x2 = x.reshape(b * s, d)
    n_qkv = 3 * ATT_WIDTH
    n_zxbc = SSM_WIDTH + SSM_WIDTH + 2 * SSM_GROUPS * SSM_STATE

    w_qkv = w_in[:, :n_qkv].astype(BF16)
    w_zxbc = w_in[:, n_qkv:n_qkv + n_zxbc].astype(BF16)
    w_dt = jnp.pad(w_in[:, n_qkv + n_zxbc:], ((0, 0), (0, LANES - SSM_HEADS))).astype(BF16)
    w_out_b = w_out.astype(BF16)
    wr = jnp.concatenate(
        [w_group, jnp.transpose(w_expert, (1, 0, 2)).reshape(d, N_EXPERTS),
         jnp.zeros((d, LANES - MOE_GROUPS - N_EXPERTS), F32)], axis=1)
    wr_hi = wr.astype(BF16)
    wr_lo = (wr - wr_hi.astype(F32)).astype(BF16)
    br = jnp.concatenate([b_group, b_expert.reshape(-1), jnp.zeros((LANES - MOE_GROUPS - N_EXPERTS,), F32)]).reshape(1, LANES)
    kk = jnp.arange(ATT_BLK)
    uo = jnp.concatenate([(kk[:, None] > kk[None, :]).astype(BF16), jnp.ones((ATT_BLK, ATT_BLK), BF16)], axis=1)

    u, dt_raw = _norm_dt(x2, norm_mix_w, w_dt)
    qkv = _proj_heads(u, w_qkv)
    zxbc = _proj(u, w_zxbc, F32, "proj_zxbc")
    att = _attention(qkv, attn_norm_w, uo)
    y = _ssd(zxbc, dt_raw, conv_w, conv_b, dt_bias, a_log, d_skip, ssd_norm_w)
    h1 = _out_proj(att, y, w_out_b, x2)

    up, route = _router(h1, norm_ffn_w, wr_hi, wr_lo, br)
    dest, row_tok, blk_first, blk_cnt, nact = _dispatch_plan(route)
    xs = _gather_rows(row_tok, nact * MOE_BLK, up, MOE_ROWS, "dispatch")
    hdn = _moe_up(blk_first, blk_cnt, nact, xs, w_gate, w_up)
    ys = _moe_down(blk_first, blk_cnt, nact, hdn, w_down)
    slot_major = dest.reshape(s, EXPERT_TOP_K).T.reshape(-1)
    n_slots = jnp.full((1,), s * EXPERT_TOP_K, I32)
    ya = _gather_rows(slot_major, n_slots, ys, s * EXPERT_TOP_K, "undispatch")
    out = _combine(h1, ya, route, norm_final_w)
    return out.reshape(b, s, d)
```

```python
import functools
import math

import jax
import jax.numpy as jnp
from jax import lax
from jax.experimental import pallas as pl
from jax.experimental.pallas import tpu as pltpu

F32 = jnp.float32
BF16 = jnp.bfloat16
U32 = jnp.uint32
I32 = jnp.int32

D_MODEL = 4096
SEQ = 8192
ATT_WIDTH = 2048
ATT_HEAD_DIM = 128
ATT_HEADS = 16
SSM_WIDTH = 2048
SSM_HEAD_DIM = 64
SSM_HEADS = 32
SSM_STATE = 128
SSM_GROUPS = 8
SSM_GROUP_WIDTH = SSM_WIDTH // SSM_GROUPS
CONV_WIDTH = 4
MOE_GROUPS = 8
EXPERTS_PER_GROUP = 8
N_EXPERTS = 64
EXPERT_TOP_K = 2
EXPERT_HIDDEN = 768
EPS = 1e-6

LANES = 128
SUBLANES = 8
VMEM_LIMIT = 56 * 1024 * 1024

NORM_ROWS = 256
MM_TM = 1024
MM_TN = 1024
ATT_BLK = 128
ATT_HPS = 4
SSD_T = 128
SSD_CCH = 256
MOE_BLK = 128
MOE_TH = 384
MOE_NBLK = -(-(SEQ * EXPERT_TOP_K + N_EXPERTS * (MOE_BLK - 1)) // MOE_BLK)
MOE_ROWS = MOE_NBLK * MOE_BLK
GATHER_BLK = 512
GATHER_UNROLL = 8
BLOCK_DMA_PRIORITY = 1
ATT_UNDERFLOW = 104.5

NEG_BIG = -1e30


def _cparams(sem):
    return pltpu.CompilerParams(dimension_semantics=sem, vmem_limit_bytes=VMEM_LIMIT)


def _softplus(x):
    return jnp.maximum(x, 0.0) + jnp.log(1.0 + jnp.exp(-jnp.abs(x)))


def _silu(x):
    return x / (1.0 + jnp.exp(-x))


def _split2(x):
    hi = x.astype(BF16)
    lo = (x - hi.astype(F32)).astype(BF16)
    return hi, lo


def _split3(x):
    h1 = x.astype(BF16)
    r1 = x - h1.astype(F32)
    h2 = r1.astype(BF16)
    h3 = (r1 - h2.astype(F32)).astype(BF16)
    return h1, h2, h3


def _dot(a, b):
    return jnp.dot(a, b, preferred_element_type=F32)


def _dot_nt(a, b):
    return lax.dot_general(a, b, (((1,), (1,)), ((), ())), preferred_element_type=F32)


def _dot_exact01(x, m, parts):
    terms = _split3(x) if parts == 3 else _split2(x)
    acc = _dot(terms[0], m)
    for t in terms[1:]:
        acc = acc + _dot(t, m)
    return acc


def _pack_bf16_pair(a, b):
    ab = pltpu.bitcast(a.astype(BF16).astype(F32), U32)
    bb = pltpu.bitcast(b.astype(BF16).astype(F32), U32)
    return ab | (bb >> 16)


def _unpack_bf16_pair(p):
    a = pltpu.bitcast(p & jnp.uint32(0xFFFF0000), F32)
    b = pltpu.bitcast(p << 16, F32)
    return a, b


def _norm_dt_kernel(x_ref, w_ref, wdt_ref, u_ref, dt_ref):
    x = x_ref[...]
    y = x * lax.rsqrt(jnp.mean(x * x, axis=-1, keepdims=True) + EPS) * w_ref[...]
    ub = y.astype(BF16)
    u_ref[...] = ub
    dt_ref[...] = _dot(ub, wdt_ref[...])


def _norm_dt(x2, norm_w, wdt_b):
    s, d = x2.shape
    return pl.pallas_call(
        _norm_dt_kernel,
        grid=(s // NORM_ROWS,),
        in_specs=[
            pl.BlockSpec((NORM_ROWS, d), lambda i: (i, 0)),
            pl.BlockSpec((1, d), lambda i: (0, 0)),
            pl.BlockSpec((d, LANES), lambda i: (0, 0)),
        ],
        out_specs=[
            pl.BlockSpec((NORM_ROWS, d), lambda i: (i, 0)),
            pl.BlockSpec((NORM_ROWS, LANES), lambda i: (i, 0)),
        ],
        out_shape=[jax.ShapeDtypeStruct((s, d), BF16), jax.ShapeDtypeStruct((s, LANES), F32)],
        compiler_params=_cparams(("parallel",)),
        name="norm_dt",
    )(x2, norm_w.reshape(1, d), wdt_b)


def _proj_heads_kernel(u_ref, w_ref, o_ref):
    acc = _dot(u_ref[...], w_ref[...])
    for hh in range(MM_TN // ATT_HEAD_DIM):
        o_ref[hh] = acc[:, hh * ATT_HEAD_DIM:(hh + 1) * ATT_HEAD_DIM].astype(o_ref.dtype)


def _proj_heads(u, w_b, col0, n):
    s, d = u.shape
    hpt = MM_TN // ATT_HEAD_DIM
    j0 = col0 // MM_TN
    return pl.pallas_call(
        _proj_heads_kernel,
        grid=(n // MM_TN, s // MM_TM),
        in_specs=[
            pl.BlockSpec((MM_TM, d), lambda j, i: (i, 0)),
            pl.BlockSpec((d, MM_TN), lambda j, i: (0, j0 + j)),
        ],
        out_specs=pl.BlockSpec((hpt, MM_TM, ATT_HEAD_DIM), lambda j, i: (j, i, 0)),
        out_shape=jax.ShapeDtypeStruct((n // ATT_HEAD_DIM, s, ATT_HEAD_DIM), BF16),
        compiler_params=_cparams(("parallel", "parallel")),
        name="proj_qkv",
    )(u, w_b)


def _proj_kernel(u_ref, w_ref, o_ref):
    o_ref[...] = _dot(u_ref[...], w_ref[...]).astype(o_ref.dtype)


def _proj(u, w_b, col0, n, out_dtype, name):
    s, d = u.shape
    j0 = col0 // MM_TN
    return pl.pallas_call(
        _proj_kernel,
        grid=(n // MM_TN, s // MM_TM),
        in_specs=[
            pl.BlockSpec((MM_TM, d), lambda j, i: (i, 0)),
            pl.BlockSpec((d, MM_TN), lambda j, i: (0, j0 + j)),
        ],
        out_specs=pl.BlockSpec((MM_TM, MM_TN), lambda j, i: (i, j)),
        out_shape=jax.ShapeDtypeStruct((s, n), out_dtype),
        compiler_params=_cparams(("parallel", "parallel")),
        name=name,
    )(u, w_b)


def _attn_kernel(q_ref, k_ref, v_ref, nw_ref, uo_ref, o_ref, *scratch):
    acc_refs, c_refs = scratch[:ATT_HPS], scratch[ATT_HPS:]
    _attn_body(q_ref, k_ref, v_ref, nw_ref, uo_ref, o_ref, acc_refs, c_refs)


def _attn_body(q_ref, k_ref, v_ref, nw_ref, uo_ref, o_ref, acc_refs, c_refs):
    i = pl.program_id(1)
    uo = uo_ref[...]
    scale = 1.0 / math.sqrt(ATT_HEAD_DIM)
    row = lax.broadcasted_iota(I32, (ATT_BLK, ATT_BLK), 0)
    col = lax.broadcasted_iota(I32, (ATT_BLK, ATT_BLK), 1)
    strict = col < row

    def block(j, masked):
        off = pl.multiple_of(j * ATT_BLK, ATT_BLK)
        heads = range(ATT_HPS)
        zs = [_dot_nt(q_ref[hh], k_ref[hh, pl.ds(off, ATT_BLK), :]) * scale for hh in heads]
        sps = [_softplus(z) for z in zs]
        log_betas = [z - sp for z, sp in zip(zs, sps)]
        log_1ms = [jnp.where(strict, -sp, 0.0) if masked else -sp for sp in sps]
        splits = [_split2(x) for x in log_1ms]
        r2s = [_dot(hi, uo) + _dot(lo, uo) for hi, lo in splits]
        cmax = None
        ws = []
        for hh in heads:
            r = r2s[hh][:, :ATT_BLK]
            tot = r2s[hh][:, ATT_BLK:]
            if masked:
                ws.append(jnp.where(strict, jnp.exp(log_betas[hh] + r), 0.0).astype(BF16))
                c_new = tot
            else:
                c_old = c_refs[hh][...]
                ws.append(jnp.exp(log_betas[hh] + r + c_old).astype(BF16))
                c_new = c_old + tot
            c_refs[hh][...] = c_new
            cmax = c_new if cmax is None else jnp.maximum(cmax, c_new)
        for hh in heads:
            pv = _dot(ws[hh], v_ref[hh, pl.ds(off, ATT_BLK), :])
            if masked:
                acc_refs[hh][...] = pv
            else:
                acc_refs[hh][...] += pv
        return jnp.max(cmax)

    def cond(carry):
        return (carry[0] < i) & (carry[1] > -ATT_UNDERFLOW)

    def body(carry):
        t = carry[0]
        return t + 1, block(i - 1 - t, False)

    lax.while_loop(cond, body, (jnp.int32(0), block(i, True)))
    for hh in range(ATT_HPS):
        acc = acc_refs[hh][...]
        y = acc * lax.rsqrt(jnp.mean(acc * acc, axis=-1, keepdims=True) + EPS) * nw_ref[hh]
        o_ref[:, hh * ATT_HEAD_DIM:(hh + 1) * ATT_HEAD_DIM] = y.astype(o_ref.dtype)


def _attention(qkv, attn_norm_w, uo):
    s = qkv.shape[1]
    ng = ATT_HEADS // ATT_HPS
    return pl.pallas_call(
        _attn_kernel,
        grid=(ng, s // ATT_BLK),
        in_specs=[
            pl.BlockSpec((ATT_HPS, ATT_BLK, ATT_HEAD_DIM), lambda g, i: (g, i, 0)),
            pl.BlockSpec((ATT_HPS, s, ATT_HEAD_DIM), lambda g, i: (ng + g, 0, 0)),
            pl.BlockSpec((ATT_HPS, s, ATT_HEAD_DIM), lambda g, i: (2 * ng + g, 0, 0)),
            pl.BlockSpec((ATT_HPS, 1, ATT_HEAD_DIM), lambda g, i: (g, 0, 0)),
            pl.BlockSpec((ATT_BLK, 2 * ATT_BLK), lambda g, i: (0, 0)),
        ],
        out_specs=pl.BlockSpec((ATT_BLK, ATT_HPS * ATT_HEAD_DIM), lambda g, i: (i, g)),
        out_shape=jax.ShapeDtypeStruct((s, ATT_WIDTH), BF16),
        scratch_shapes=(
            [pltpu.VMEM((ATT_BLK, ATT_HEAD_DIM), F32) for _ in range(ATT_HPS)]
            + [pltpu.VMEM((ATT_BLK, ATT_BLK), F32) for _ in range(ATT_HPS)]
        ),
        compiler_params=_cparams(("parallel", "arbitrary")),
        name="sb_attn",
    )(qkv, qkv, qkv, attn_norm_w.reshape(ATT_HEADS, 1, ATT_HEAD_DIM), uo)


def _ssd_kernel(z_ref, xs_ref, bc_ref, dt_ref, cw_ref, cb_ref, dtb_ref, alog_ref, dskip_ref, nw_ref,
                tri_ref, eh_ref, et_ref, o_ref,
                xbuf, bbuf, xc, bcc, acol, acst, dtf, decf, expf, state):
    c = pl.program_id(0)
    t = SSD_T
    w2 = SSM_WIDTH

    @pl.when(c == 0)
    def _():
        xbuf[0:SUBLANES, :] = jnp.zeros((SUBLANES, w2), F32)
        bbuf[0:SUBLANES, :] = jnp.zeros((SUBLANES, w2), F32)
        state[...] = jnp.zeros_like(state)

    def conv_chunk(ci, carry):
        for src_ref, buf, dst, woff in ((xs_ref, xbuf, xc, 0), (bc_ref, bbuf, bcc, w2)):
            cols = pl.ds(pl.multiple_of(ci * SSD_CCH, SSD_CCH), SSD_CCH)
            wcols = pl.ds(pl.multiple_of(woff + ci * SSD_CCH, SSD_CCH), SSD_CCH)
            cur = src_ref[:, cols]
            buf[SUBLANES:SUBLANES + t, cols] = cur
            acc = cur * cw_ref[CONV_WIDTH - 1:CONV_WIDTH, wcols] + cb_ref[:, wcols]
            for k in range(CONV_WIDTH - 1):
                start = SUBLANES - (CONV_WIDTH - 1) + k
                acc = acc + buf[start:start + t, cols] * cw_ref[k:k + 1, wcols]
            dst[:, cols] = _silu(acc)
            buf[0:SUBLANES, cols] = cur[t - SUBLANES:, :]
        return carry

    lax.fori_loop(0, w2 // SSD_CCH, conv_chunk, 0)

    dt = _softplus(dt_ref[...] + dtb_ref[...])
    a = -jnp.exp(alog_ref[...])
    a_cs = _dot_exact01_lhs(tri_ref[...], dt * a)
    a_last = a_cs[t - 1:t, :]
    terms = []
    for v in (dt, dt * jnp.exp(a_last - a_cs), jnp.exp(a_cs)):
        terms.extend(_split2(v))
    ex = _dot(jnp.concatenate(terms, axis=0), eh_ref[...])
    dtf[...] = ex[0:t] + ex[t:2 * t]
    decf[...] = ex[2 * t:3 * t] + ex[3 * t:4 * t]
    expf[...] = ex[4 * t:5 * t] + ex[5 * t:6 * t]
    ac = _dot(jnp.concatenate(_split3(a_cs), axis=0), et_ref[...])
    acol[...] = ac[0:t] + ac[t:2 * t] + ac[2 * t:3 * t]
    acst[...] = a_cs.T

    row = lax.broadcasted_iota(I32, (t, t), 0)
    col = lax.broadcasted_iota(I32, (t, t), 1)
    lower = col <= row
    head_of_lane = lax.broadcasted_iota(I32, (t, SSM_GROUP_WIDTH), 1) // SSM_HEAD_DIM

    def group(g, carry):
        go = pl.multiple_of(g * SSM_GROUP_WIDTH, SSM_GROUP_WIDTH)
        no = pl.multiple_of(g * SSM_STATE, SSM_STATE)
        bg = bcc[:, pl.ds(no, SSM_STATE)]
        cg = bcc[:, pl.ds(SSM_GROUPS * SSM_STATE + no, SSM_STATE)]
        cgb = cg.astype(BF16)
        cb = _dot_nt(cgb, bg.astype(BF16))
        xg = xc[:, pl.ds(go, SSM_GROUP_WIDTH)]
        xd = xg * dtf[:, pl.ds(go, SSM_GROUP_WIDTH)]
        xdec = (xg * decf[:, pl.ds(go, SSM_GROUP_WIDTH)]).astype(BF16)
        st = state[g]
        y = _dot(cgb, st.astype(BF16)) * expf[:, pl.ds(go, SSM_GROUP_WIDTH)]
        for r in range(SSM_GROUP_WIDTH // SSM_HEAD_DIM):
            h = g * (SSM_GROUP_WIDTH // SSM_HEAD_DIM) + r
            ho = pl.multiple_of(h * t, t)
            seg = acol[:, pl.ds(ho, t)] - acst[pl.ds(h, 1), :]
            lmat = jnp.exp(jnp.where(lower, seg, NEG_BIG))
            gm = (cb * lmat).astype(BF16)
            xd_r = jnp.where(head_of_lane == r, xd, 0.0).astype(BF16)
            y = y + _dot(gm, xd_r)
        state[g] = st * expf[t - 1:t, pl.ds(go, SSM_GROUP_WIDTH)] + _dot(bg.T.astype(BF16), xdec)
        y = y + dskip_ref[:, pl.ds(go, SSM_GROUP_WIDTH)] * xg
        y = y * _silu(z_ref[:, pl.ds(go, SSM_GROUP_WIDTH)])
        y = y * lax.rsqrt(jnp.mean(y * y, axis=-1, keepdims=True) + EPS) * nw_ref[:, pl.ds(go, SSM_GROUP_WIDTH)]
        o_ref[:, pl.ds(go, SSM_GROUP_WIDTH)] = y.astype(o_ref.dtype)
        return carry

    lax.fori_loop(0, SSM_GROUPS, group, 0)


def _dot_exact01_lhs(m, x):
    h1, h2, h3 = _split3(x)
    return _dot(m, h1) + _dot(m, h2) + _dot(m, h3)


def _ssd(zxbc, dt_raw, conv_w, conv_b, dt_bias, a_log, d_skip, ssd_norm_w):
    s = zxbc.shape[0]
    t = SSD_T
    w2 = SSM_WIDTH
    pad = LANES - SSM_HEADS
    tri = (jnp.arange(t)[:, None] >= jnp.arange(t)[None, :]).astype(BF16)
    heads = jnp.arange(LANES)[:, None]
    eh = (heads == (jnp.arange(w2)[None, :] // SSM_HEAD_DIM)).astype(BF16)
    et = (heads == (jnp.arange(SSM_HEADS * t)[None, :] // t)).astype(BF16)
    dtb = jnp.pad(dt_bias.astype(F32), (0, pad)).reshape(1, LANES)
    alog = jnp.pad(a_log.astype(F32), (0, pad)).reshape(1, LANES)
    dskip = jnp.repeat(d_skip.astype(F32), SSM_HEAD_DIM).reshape(1, w2)
    const = lambda c: (0, 0)
    return pl.pallas_call(
        _ssd_kernel,
        grid=(s // t,),
        in_specs=[
            pl.BlockSpec((t, w2), lambda c: (c, 0)),
            pl.BlockSpec((t, w2), lambda c: (c, 1)),
            pl.BlockSpec((t, w2), lambda c: (c, 2)),
            pl.BlockSpec((t, LANES), lambda c: (c, 0)),
            pl.BlockSpec((CONV_WIDTH, 2 * w2), const),
            pl.BlockSpec((1, 2 * w2), const),
            pl.BlockSpec((1, LANES), const),
            pl.BlockSpec((1, LANES), const),
            pl.BlockSpec((1, w2), const),
            pl.BlockSpec((1, w2), const),
            pl.BlockSpec((t, t), const),
            pl.BlockSpec((LANES, w2), const),
            pl.BlockSpec((LANES, SSM_HEADS * t), const),
        ],
        out_specs=pl.BlockSpec((t, w2), lambda c: (c, 0)),
        out_shape=jax.ShapeDtypeStruct((s, w2), BF16),
        scratch_shapes=[
            pltpu.VMEM((t + SUBLANES, w2), F32),
            pltpu.VMEM((t + SUBLANES, w2), F32),
            pltpu.VMEM((t, w2), F32),
            pltpu.VMEM((t, w2), F32),
            pltpu.VMEM((t, SSM_HEADS * t), F32),
            pltpu.VMEM((LANES, t), F32),
            pltpu.VMEM((t, w2), F32),
            pltpu.VMEM((t, w2), F32),
            pltpu.VMEM((t, w2), F32),
            pltpu.VMEM((SSM_GROUPS, SSM_STATE, SSM_GROUP_WIDTH), F32),
        ],
        compiler_params=_cparams(("arbitrary",)),
        name="ssd",
    )(zxbc, zxbc, zxbc, dt_raw, conv_w.astype(F32), conv_b.astype(F32).reshape(1, 2 * w2), dtb, alog, dskip,
      ssd_norm_w.astype(F32).reshape(1, w2), tri, eh, et)


def _out_proj_kernel(att_ref, y_ref, w_ref, x_ref, o_ref):
    acc = _dot(att_ref[...], w_ref[0:ATT_WIDTH, :])
    acc = acc + _dot(y_ref[...], w_ref[ATT_WIDTH:, :])
    o_ref[...] = x_ref[...] + acc


def _out_proj(att, y, w_b, x2):
    s, d = x2.shape
    tm, tn = 512, MM_TN
    return pl.pallas_call(
        _out_proj_kernel,
        grid=(d // tn, s // tm),
        in_specs=[
            pl.BlockSpec((tm, ATT_WIDTH), lambda j, i: (i, 0)),
            pl.BlockSpec((tm, SSM_WIDTH), lambda j, i: (i, 0)),
            pl.BlockSpec((ATT_WIDTH + SSM_WIDTH, tn), lambda j, i: (0, j)),
            pl.BlockSpec((tm, tn), lambda j, i: (i, j)),
        ],
        out_specs=pl.BlockSpec((tm, tn), lambda j, i: (i, j)),
        out_shape=jax.ShapeDtypeStruct((s, d), F32),
        compiler_params=_cparams(("parallel", "parallel")),
        name="out_proj",
    )(att, y, w_b, x2)


def _router_kernel(h_ref, w_ref, whi_ref, wlo_ref, b_ref, up_ref, route_ref):
    x = h_ref[...]
    u = x * lax.rsqrt(jnp.mean(x * x, axis=-1, keepdims=True) + EPS) * w_ref[...]
    half = D_MODEL // 2
    up_ref[...] = _pack_bf16_pair(u[:, :half], u[:, half:])
    uh, ul = _split2(u)
    whi = whi_ref[...]
    logits = _dot(uh, whi) + _dot(ul, whi) + _dot(uh, wlo_ref[...]) + b_ref[...]
    lane = lax.broadcasted_iota(I32, logits.shape, 1).astype(F32)
    ninf = -jnp.inf
    nolane = float(LANES)
    gl = jnp.where(lane < MOE_GROUPS, logits, ninf)
    gmax = jnp.max(gl, axis=-1, keepdims=True)
    gsum = jnp.sum(jnp.exp(gl - gmax), axis=-1, keepdims=True)
    g_w = 1.0 / gsum
    g_idx = jnp.min(jnp.where(gl == gmax, lane, nolane), axis=-1, keepdims=True)
    lo_lane = MOE_GROUPS + g_idx * EXPERTS_PER_GROUP
    el = jnp.where((lane >= lo_lane) & (lane < lo_lane + EXPERTS_PER_GROUP), logits, ninf)
    m0 = jnp.max(el, axis=-1, keepdims=True)
    i0 = jnp.min(jnp.where(el == m0, lane, nolane), axis=-1, keepdims=True)
    el2 = jnp.where(lane == i0, ninf, el)
    m1 = jnp.max(el2, axis=-1, keepdims=True)
    i1 = jnp.min(jnp.where(el2 == m1, lane, nolane), axis=-1, keepdims=True)
    tt = jnp.exp(m1 - m0)
    w0 = g_w / (1.0 + tt)
    w1 = g_w * tt / (1.0 + tt)
    e0 = i0 - MOE_GROUPS
    e1 = i1 - MOE_GROUPS
    out = jnp.where(lane == 0, e0, jnp.where(lane == 1, e1, jnp.where(lane == 2, w0, jnp.where(lane == 3, w1, 0.0))))
    route_ref[...] = out


def _router(h1, norm_w, wr_hi, wr_lo, br):
    s, d = h1.shape
    return pl.pallas_call(
        _router_kernel,
        grid=(s // NORM_ROWS,),
        in_specs=[
            pl.BlockSpec((NORM_ROWS, d), lambda i: (i, 0)),
            pl.BlockSpec((1, d), lambda i: (0, 0)),
            pl.BlockSpec((d, LANES), lambda i: (0, 0)),
            pl.BlockSpec((d, LANES), lambda i: (0, 0)),
            pl.BlockSpec((1, LANES), lambda i: (0, 0)),
        ],
        out_specs=[
            pl.BlockSpec((NORM_ROWS, d // 2), lambda i: (i, 0)),
            pl.BlockSpec((NORM_ROWS, LANES), lambda i: (i, 0)),
        ],
        out_shape=[jax.ShapeDtypeStruct((s, d // 2), U32), jax.ShapeDtypeStruct((s, LANES), F32)],
        compiler_params=_cparams(("parallel",)),
        name="router",
    )(h1, norm_w.reshape(1, d), wr_hi, wr_lo, br)


def _gather_rows_kernel(idx_ref, nvalid_ref, src_ref, o_ref, sem):
    base = pl.program_id(0) * GATHER_BLK

    def copy(r):
        return pltpu.make_async_copy(src_ref.at[pl.ds(idx_ref[base + r], 1)], o_ref.at[pl.ds(r, 1)], sem)

    @pl.when(base < nvalid_ref[0])
    def _():
        def issue(g, carry):
            for k in range(GATHER_UNROLL):
                copy(g * GATHER_UNROLL + k).start(priority=k % 2)
            return carry

        lax.fori_loop(0, GATHER_BLK // GATHER_UNROLL, issue, 0)

        def drain(r, carry):
            copy(r).wait()
            return carry

        lax.fori_loop(0, GATHER_BLK, drain, 0, unroll=GATHER_UNROLL)

    @pl.when(base >= nvalid_ref[0])
    def _():
        o_ref[...] = jnp.zeros_like(o_ref)


def _gather_rows(idx, nvalid, src, n_out, name):
    return pl.pallas_call(
        _gather_rows_kernel,
        grid_spec=pltpu.PrefetchScalarGridSpec(
            num_scalar_prefetch=2,
            grid=(n_out // GATHER_BLK,),
            in_specs=[pl.BlockSpec(memory_space=pl.ANY)],
            out_specs=pl.BlockSpec((GATHER_BLK, src.shape[1]), lambda b, idx_ref, nv_ref: (b, 0)),
            scratch_shapes=[pltpu.SemaphoreType.DMA(())],
        ),
        out_shape=jax.ShapeDtypeStruct((n_out, src.shape[1]), src.dtype),
        compiler_params=_cparams(("arbitrary",)),
        name=name,
    )(idx, nvalid, src)


def _zero_fill_blocks(zbuf, dst_hbm, first_blk, sem):
    zbuf[...] = jnp.zeros_like(zbuf)

    def fill(b, carry):
        r0 = pl.multiple_of(b * MOE_BLK, MOE_BLK)
        cp = pltpu.make_async_copy(zbuf, dst_hbm.at[pl.ds(r0, MOE_BLK)], sem)
        cp.start()
        cp.wait()
        return carry

    lax.fori_loop(first_blk, MOE_NBLK, fill, 0)


def _walk_blocks(n, in_copy, out_copy, compute):
    @pl.when(n > 0)
    def _():
        in_copy(0, 0).start(priority=BLOCK_DMA_PRIORITY)

    def body(j, carry):
        slot = j % 2
        in_copy(j, slot).wait()

        @pl.when(j + 1 < n)
        def _():
            in_copy(j + 1, 1 - slot).start(priority=BLOCK_DMA_PRIORITY)

        @pl.when(j >= 2)
        def _():
            out_copy(j - 2, slot).wait()

        compute(slot)
        out_copy(j, slot).start()
        return carry

    lax.fori_loop(0, n, body, 0)

    @pl.when(n >= 2)
    def _():
        out_copy(n - 2, n % 2).wait()

    @pl.when(n >= 1)
    def _():
        out_copy(n - 1, (n + 1) % 2).wait()


def _moe_up_kernel(first_ref, cnt_ref, nact_ref, xs_hbm, wg_ref, wu_ref, hdn_hbm,
                   wgb, wub, xbuf, hbuf, zbuf, in_sem, out_sem, z_sem):
    e = pl.program_id(0)
    ht = pl.program_id(1)
    n = cnt_ref[e]
    b0 = first_ref[e]
    col0 = pl.multiple_of(ht * MOE_TH, LANES)

    def rows(j):
        return pl.ds(pl.multiple_of((b0 + j) * MOE_BLK, MOE_BLK), MOE_BLK)

    def in_copy(j, slot):
        return pltpu.make_async_copy(xs_hbm.at[rows(j)], xbuf.at[slot], in_sem.at[slot])

    def out_copy(j, slot):
        return pltpu.make_async_copy(hbuf.at[slot], hdn_hbm.at[rows(j), pl.ds(col0, MOE_TH)], out_sem.at[slot])

    @pl.when((e == 0) & (ht == 0))
    def _():
        _zero_fill_blocks(zbuf, hdn_hbm, nact_ref[0], z_sem)

    @pl.when(n > 0)
    def _():
        wgb[...] = wg_ref[0].astype(BF16)
        wub[...] = wu_ref[0].astype(BF16)

    def compute(slot):
        xa, xb = _unpack_bf16_pair(xbuf[slot])
        x = jnp.concatenate([xa.astype(BF16), xb.astype(BF16)], axis=-1)
        g = _dot(x, wgb[...])
        u = _dot(x, wub[...])
        hbuf[slot] = (_silu(g) * u).astype(BF16)

    _walk_blocks(n, in_copy, out_copy, compute)


def _moe_up(blk_first, blk_cnt, nact, xs, w_gate, w_up):
    nht = EXPERT_HIDDEN // MOE_TH
    half = D_MODEL // 2
    return pl.pallas_call(
        _moe_up_kernel,
        grid_spec=pltpu.PrefetchScalarGridSpec(
            num_scalar_prefetch=3,
            grid=(N_EXPERTS, nht),
            in_specs=[
                pl.BlockSpec(memory_space=pl.ANY),
                pl.BlockSpec((1, D_MODEL, MOE_TH), lambda e, ht, bf, bc, na: (e, 0, ht)),
                pl.BlockSpec((1, D_MODEL, MOE_TH), lambda e, ht, bf, bc, na: (e, 0, ht)),
            ],
            out_specs=pl.BlockSpec(memory_space=pl.ANY),
            scratch_shapes=[
                pltpu.VMEM((D_MODEL, MOE_TH), BF16),
                pltpu.VMEM((D_MODEL, MOE_TH), BF16),
                pltpu.VMEM((2, MOE_BLK, half), U32),
                pltpu.VMEM((2, MOE_BLK, MOE_TH), BF16),
                pltpu.VMEM((MOE_BLK, EXPERT_HIDDEN), BF16),
                pltpu.SemaphoreType.DMA((2,)),
                pltpu.SemaphoreType.DMA((2,)),
                pltpu.SemaphoreType.DMA(()),
            ],
        ),
        out_shape=jax.ShapeDtypeStruct((MOE_ROWS, EXPERT_HIDDEN), BF16),
        compiler_params=_cparams(("arbitrary", "arbitrary")),
        name="moe_up",
    )(blk_first, blk_cnt, nact, xs, w_gate, w_up)


def _moe_down_kernel(first_ref, cnt_ref, nact_ref, hdn_hbm, wd_ref, ys_hbm,
                     wdb, hbuf, ybuf, zbuf, in_sem, out_sem, z_sem):
    e = pl.program_id(0)
    n = cnt_ref[e]
    b0 = first_ref[e]
    half = D_MODEL // 2

    def rows(j):
        return pl.ds(pl.multiple_of((b0 + j) * MOE_BLK, MOE_BLK), MOE_BLK)

    def in_copy(j, slot):
        return pltpu.make_async_copy(hdn_hbm.at[rows(j)], hbuf.at[slot], in_sem.at[slot])

    def out_copy(j, slot):
        return pltpu.make_async_copy(ybuf.at[slot], ys_hbm.at[rows(j)], out_sem.at[slot])

    @pl.when(e == 0)
    def _():
        _zero_fill_blocks(zbuf, ys_hbm, nact_ref[0], z_sem)

    @pl.when(n > 0)
    def _():
        wdb[...] = wd_ref[0].astype(BF16)

    def compute(slot):
        y = _dot(hbuf[slot], wdb[...])
        ybuf[slot] = _pack_bf16_pair(y[:, :half], y[:, half:])

    _walk_blocks(n, in_copy, out_copy, compute)


def _moe_down(blk_first, blk_cnt, nact, hdn, w_down):
    half = D_MODEL // 2
    return pl.pallas_call(
        _moe_down_kernel,
        grid_spec=pltpu.PrefetchScalarGridSpec(
            num_scalar_prefetch=3,
            grid=(N_EXPERTS,),
            in_specs=[
                pl.BlockSpec(memory_space=pl.ANY),
                pl.BlockSpec((1, EXPERT_HIDDEN, D_MODEL), lambda e, bf, bc, na: (e, 0, 0)),
            ],
            out_specs=pl.BlockSpec(memory_space=pl.ANY),
            scratch_shapes=[
                pltpu.VMEM((EXPERT_HIDDEN, D_MODEL), BF16),
                pltpu.VMEM((2, MOE_BLK, EXPERT_HIDDEN), BF16),
                pltpu.VMEM((2, MOE_BLK, half), U32),
                pltpu.VMEM((MOE_BLK, half), U32),
                pltpu.SemaphoreType.DMA((2,)),
                pltpu.SemaphoreType.DMA((2,)),
                pltpu.SemaphoreType.DMA(()),
            ],
        ),
        out_shape=jax.ShapeDtypeStruct((MOE_ROWS, half), U32),
        compiler_params=_cparams(("arbitrary",)),
        name="moe_down",
    )(blk_first, blk_cnt, nact, hdn, w_down)


def _combine_kernel(h_ref, ya0_ref, ya1_ref, route_ref, w_ref, o_ref):
    route = route_ref[...]
    w0 = route[:, 2:3]
    w1 = route[:, 3:4]
    a0, b0 = _unpack_bf16_pair(ya0_ref[...])
    a1, b1 = _unpack_bf16_pair(ya1_ref[...])
    moe = jnp.concatenate([w0 * a0 + w1 * a1, w0 * b0 + w1 * b1], axis=-1)
    h = h_ref[...] + moe
    o_ref[...] = h * lax.rsqrt(jnp.mean(h * h, axis=-1, keepdims=True) + EPS) * w_ref[...]


def _combine(h1, ya, route, norm_w):
    s, d = h1.shape
    nb = s // NORM_ROWS
    return pl.pallas_call(
        _combine_kernel,
        grid=(nb,),
        in_specs=[
            pl.BlockSpec((NORM_ROWS, d), lambda i: (i, 0)),
            pl.BlockSpec((NORM_ROWS, d // 2), lambda i: (i, 0)),
            pl.BlockSpec((NORM_ROWS, d // 2), lambda i: (i + nb, 0)),
            pl.BlockSpec((NORM_ROWS, LANES), lambda i: (i, 0)),
            pl.BlockSpec((1, d), lambda i: (0, 0)),
        ],
        out_specs=pl.BlockSpec((NORM_ROWS, d), lambda i: (i, 0)),
        out_shape=jax.ShapeDtypeStruct((s, d), F32),
        compiler_params=_cparams(("parallel",)),
        name="combine",
    )(h1, ya, ya, route, norm_w.reshape(1, d))


def _dispatch_plan(route):
    s = route.shape[0]
    flat_e = route[:, :EXPERT_TOP_K].astype(I32).reshape(-1)
    onehot = (flat_e[:, None] == jnp.arange(N_EXPERTS, dtype=I32)[None, :]).astype(I32)
    csum = jnp.cumsum(onehot, axis=0)
    rank = jnp.sum((csum - 1) * onehot, axis=1)
    counts = csum[-1]
    padded = (counts + MOE_BLK - 1) // MOE_BLK * MOE_BLK
    pend = jnp.cumsum(padded)
    pstart = pend - padded
    dest = (pstart[flat_e] + rank).astype(I32)
    flat_t = jnp.repeat(jnp.arange(s, dtype=I32), EXPERT_TOP_K)
    row_tok = (jnp.arange(MOE_ROWS, dtype=I32) % s).at[dest].set(flat_t)
    nact = (pend[-1] // MOE_BLK).astype(I32).reshape(1)
    blk_first = (pstart // MOE_BLK).astype(I32)
    blk_cnt = (padded // MOE_BLK).astype(I32)
    return dest, row_tok, blk_first, blk_cnt, nact


def kernel(x, norm_mix_w, w_in, conv_w, conv_b, dt_bias, a_log, d_skip, ssd_norm_w, attn_norm_w, w_out,
           norm_ffn_w, w_group, b_group, w_expert, b_expert, w_gate, w_up, w_down, norm_final_w):
    b, s, d = x.shape
    x2 = x.reshape(b * s, d)
    n_qkv = 3 * ATT_WIDTH
    n_zxbc = SSM_WIDTH + SSM_WIDTH + 2 * SSM_GROUPS * SSM_STATE

    w_in_b = w_in.astype(BF16)
    w_dt = jnp.pad(w_in_b[:, n_qkv + n_zxbc:], ((0, 0), (0, LANES - SSM_HEADS)))
    w_out_b = w_out.astype(BF16)
    wr = jnp.concatenate(
        [w_group, jnp.transpose(w_expert, (1, 0, 2)).reshape(d, N_EXPERTS),
         jnp.zeros((d, LANES - MOE_GROUPS - N_EXPERTS), F32)], axis=1)
    wr_hi = wr.astype(BF16)
    wr_lo = (wr - wr_hi.astype(F32)).astype(BF16)
    br = jnp.concatenate([b_group, b_expert.reshape(-1), jnp.zeros((LANES - MOE_GROUPS - N_EXPERTS,), F32)]).reshape(1, LANES)
    kk = jnp.arange(ATT_BLK)
    uo = jnp.concatenate([(kk[:, None] > kk[None, :]).astype(BF16), jnp.ones((ATT_BLK, ATT_BLK), BF16)], axis=1)

    u, dt_raw = _norm_dt(x2, norm_mix_w, w_dt)
    qkv = _proj_heads(u, w_in_b, 0, n_qkv)
    zxbc = _proj(u, w_in_b, n_qkv, n_zxbc, F32, "proj_zxbc")
    att = _attention(qkv, attn_norm_w, uo)
    y = _ssd(zxbc, dt_raw, conv_w, conv_b, dt_bias, a_log, d_skip, ssd_norm_w)
    h1 = _out_proj(att, y, w_out_b, x2)

    up, route = _router(h1, norm_ffn_w, wr_hi, wr_lo, br)
    dest, row_tok, blk_first, blk_cnt, nact = _dispatch_plan(route)
    xs = _gather_rows(row_tok, nact * MOE_BLK, up, MOE_ROWS, "dispatch")
    hdn = _moe_up(blk_first, blk_cnt, nact, xs, w_gate, w_up)
    ys = _moe_down(blk_first, blk_cnt, nact, hdn, w_down)
    slot_major = dest.reshape(s, EXPERT_TOP_K).T.reshape(-1)
    n_slots = jnp.full((1,), s * EXPERT_TOP_K, I32)
    ya = _gather_rows(slot_major, n_slots, ys, s * EXPERT_TOP_K, "undispatch")
    out = _combine(h1, ya, route, norm_final_w)
    return out.reshape(b, s, d)
```

```python
import functools
import math

import jax
import jax.numpy as jnp
from jax import lax
from jax.experimental import pallas as pl
from jax.experimental.pallas import tpu as pltpu

F32 = jnp.float32
BF16 = jnp.bfloat16
U32 = jnp.uint32
I32 = jnp.int32

D_MODEL = 4096
SEQ = 8192
ATT_WIDTH = 2048
ATT_HEAD_DIM = 128
ATT_HEADS = 16
SSM_WIDTH = 2048
SSM_HEAD_DIM = 64
SSM_HEADS = 32
SSM_STATE = 128
SSM_GROUPS = 8
SSM_GROUP_WIDTH = SSM_WIDTH // SSM_GROUPS
CONV_WIDTH = 4
MOE_GROUPS = 8
EXPERTS_PER_GROUP = 8
N_EXPERTS = 64
EXPERT_TOP_K = 2
EXPERT_HIDDEN = 768
EPS = 1e-6

LANES = 128
SUBLANES = 8
VMEM_LIMIT = 56 * 1024 * 1024

NORM_ROWS = 256
MM_TM = 1024
MM_TN = 1024
ATT_BLK = 128
ATT_HPS = 4
SSD_T = 128
SSD_CCH = 256
MOE_BLK = 128
MOE_TH = 384
MOE_NBLK = -(-(SEQ * EXPERT_TOP_K + N_EXPERTS * (MOE_BLK - 1)) // MOE_BLK)
MOE_ROWS = MOE_NBLK * MOE_BLK
GATHER_BLK = 512
GATHER_UNROLL = 8
BLOCK_DMA_PRIORITY = 1
ATT_UNDERFLOW = 104.5

NEG_BIG = -1e30


def _cparams(sem):
    return pltpu.CompilerParams(dimension_semantics=sem, vmem_limit_bytes=VMEM_LIMIT)


def _softplus(x):
    return jnp.maximum(x, 0.0) + jnp.log(1.0 + jnp.exp(-jnp.abs(x)))


def _silu(x):
    return x / (1.0 + jnp.exp(-x))


def _split2(x):
    hi = x.astype(BF16)
    lo = (x - hi.astype(F32)).astype(BF16)
    return hi, lo


def _split3(x):
    h1 = x.astype(BF16)
    r1 = x - h1.astype(F32)
    h2 = r1.astype(BF16)
    h3 = (r1 - h2.astype(F32)).astype(BF16)
    return h1, h2, h3


def _dot(a, b):
    return jnp.dot(a, b, preferred_element_type=F32)


def _dot_nt(a, b):
    return lax.dot_general(a, b, (((1,), (1,)), ((), ())), preferred_element_type=F32)


def _dot_exact01(x, m, parts):
    terms = _split3(x) if parts == 3 else _split2(x)
    acc = _dot(terms[0], m)
    for t in terms[1:]:
        acc = acc + _dot(t, m)
    return acc


def _pack_bf16_pair(a, b):
    ab = pltpu.bitcast(a.astype(BF16).astype(F32), U32)
    bb = pltpu.bitcast(b.astype(BF16).astype(F32), U32)
    return ab | (bb >> 16)


def _unpack_bf16_pair(p):
    a = pltpu.bitcast(p & jnp.uint32(0xFFFF0000), F32)
    b = pltpu.bitcast(p << 16, F32)
    return a, b


def _norm_dt_kernel(x_ref, w_ref, wdt_ref, u_ref, dt_ref):
    x = x_ref[...]
    y = x * lax.rsqrt(jnp.mean(x * x, axis=-1, keepdims=True) + EPS) * w_ref[...]
    ub = y.astype(BF16)
    u_ref[...] = ub
    dt_ref[...] = _dot(ub, wdt_ref[...])


def _norm_dt(x2, norm_w, wdt_b):
    s, d = x2.shape
    return pl.pallas_call(
        _norm_dt_kernel,
        grid=(s // NORM_ROWS,),
        in_specs=[
            pl.BlockSpec((NORM_ROWS, d), lambda i: (i, 0)),
            pl.BlockSpec((1, d), lambda i: (0, 0)),
            pl.BlockSpec((d, LANES), lambda i: (0, 0)),
        ],
        out_specs=[
            pl.BlockSpec((NORM_ROWS, d), lambda i: (i, 0)),
            pl.BlockSpec((NORM_ROWS, LANES), lambda i: (i, 0)),
        ],
        out_shape=[jax.ShapeDtypeStruct((s, d), BF16), jax.ShapeDtypeStruct((s, LANES), F32)],
        compiler_params=_cparams(("parallel",)),
        name="norm_dt",
    )(x2, norm_w.reshape(1, d), wdt_b)


def _proj_heads_kernel(u_ref, w_ref, o_ref):
    acc = _dot(u_ref[...], w_ref[...])
    for hh in range(MM_TN // ATT_HEAD_DIM):
        o_ref[hh] = acc[:, hh * ATT_HEAD_DIM:(hh + 1) * ATT_HEAD_DIM].astype(o_ref.dtype)


def _proj_heads(u, w_b, col0, n):
    s, d = u.shape
    hpt = MM_TN // ATT_HEAD_DIM
    j0 = col0 // MM_TN
    return pl.pallas_call(
        _proj_heads_kernel,
        grid=(n // MM_TN, s // MM_TM),
        in_specs=[
            pl.BlockSpec((MM_TM, d), lambda j, i: (i, 0)),
            pl.BlockSpec((d, MM_TN), lambda j, i: (0, j0 + j)),
        ],
        out_specs=pl.BlockSpec((hpt, MM_TM, ATT_HEAD_DIM), lambda j, i: (j, i, 0)),
        out_shape=jax.ShapeDtypeStruct((n // ATT_HEAD_DIM, s, ATT_HEAD_DIM), BF16),
        compiler_params=_cparams(("parallel", "parallel")),
        name="proj_qkv",
    )(u, w_b)


def _proj_kernel(u_ref, w_ref, o_ref):
    o_ref[...] = _dot(u_ref[...], w_ref[...]).astype(o_ref.dtype)


def _proj(u, w_b, col0, n, out_dtype, name):
    s, d = u.shape
    j0 = col0 // MM_TN
    return pl.pallas_call(
        _proj_kernel,
        grid=(n // MM_TN, s // MM_TM),
        in_specs=[
            pl.BlockSpec((MM_TM, d), lambda j, i: (i, 0)),
            pl.BlockSpec((d, MM_TN), lambda j, i: (0, j0 + j)),
        ],
        out_specs=pl.BlockSpec((MM_TM, MM_TN), lambda j, i: (i, j)),
        out_shape=jax.ShapeDtypeStruct((s, n), out_dtype),
        compiler_params=_cparams(("parallel", "parallel")),
        name=name,
    )(u, w_b)


def _attn_kernel(q_ref, k_ref, v_ref, nw_ref, uo_ref, o_ref, *scratch):
    acc_refs, c_refs = scratch[:ATT_HPS], scratch[ATT_HPS:]
    _attn_body(q_ref, k_ref, v_ref, nw_ref, uo_ref, o_ref, acc_refs, c_refs)


def _attn_body(q_ref, k_ref, v_ref, nw_ref, uo_ref, o_ref, acc_refs, c_refs):
    i = pl.program_id(1)
    uo = uo_ref[...]
    scale = 1.0 / math.sqrt(ATT_HEAD_DIM)
    row = lax.broadcasted_iota(I32, (ATT_BLK, ATT_BLK), 0)
    col = lax.broadcasted_iota(I32, (ATT_BLK, ATT_BLK), 1)
    strict = col < row

    def block(j, masked):
        off = pl.multiple_of(j * ATT_BLK, ATT_BLK)
        heads = range(ATT_HPS)
        zs = [_dot_nt(q_ref[hh], k_ref[hh, pl.ds(off, ATT_BLK), :]) * scale for hh in heads]
        sps = [_softplus(z) for z in zs]
        log_betas = [z - sp for z, sp in zip(zs, sps)]
        log_1ms = [jnp.where(strict, -sp, 0.0) if masked else -sp for sp in sps]
        splits = [_split2(x) for x in log_1ms]
        r2s = [_dot(hi, uo) + _dot(lo, uo) for hi, lo in splits]
        cmax = None
        ws = []
        for hh in heads:
            r = r2s[hh][:, :ATT_BLK]
            tot = r2s[hh][:, ATT_BLK:]
            if masked:
                ws.append(jnp.where(strict, jnp.exp(log_betas[hh] + r), 0.0).astype(BF16))
                c_new = tot
            else:
                c_old = c_refs[hh][...]
                ws.append(jnp.exp(log_betas[hh] + r + c_old).astype(BF16))
                c_new = c_old + tot
            c_refs[hh][...] = c_new
            cmax = c_new if cmax is None else jnp.maximum(cmax, c_new)
        for hh in heads:
            pv = _dot(ws[hh], v_ref[hh, pl.ds(off, ATT_BLK), :])
            if masked:
                acc_refs[hh][...] = pv
            else:
                acc_refs[hh][...] += pv
        return jnp.max(cmax)

    def cond(carry):
        return (carry[0] < i) & (carry[1] > -ATT_UNDERFLOW)

    def body(carry):
        t = carry[0]
        return t + 1, block(i - 1 - t, False)

    lax.while_loop(cond, body, (jnp.int32(0), block(i, True)))
    for hh in range(ATT_HPS):
        acc = acc_refs[hh][...]
        y = acc * lax.rsqrt(jnp.mean(acc * acc, axis=-1, keepdims=True) + EPS) * nw_ref[hh]
        o_ref[:, hh * ATT_HEAD_DIM:(hh + 1) * ATT_HEAD_DIM] = y.astype(o_ref.dtype)


def _attention(qkv, attn_norm_w, uo):
    s = qkv.shape[1]
    ng = ATT_HEADS // ATT_HPS
    return pl.pallas_call(
        _attn_kernel,
        grid=(ng, s // ATT_BLK),
        in_specs=[
            pl.BlockSpec((ATT_HPS, ATT_BLK, ATT_HEAD_DIM), lambda g, i: (g, i, 0)),
            pl.BlockSpec((ATT_HPS, s, ATT_HEAD_DIM), lambda g, i: (ng + g, 0, 0)),
            pl.BlockSpec((ATT_HPS, s, ATT_HEAD_DIM), lambda g, i: (2 * ng + g, 0, 0)),
            pl.BlockSpec((ATT_HPS, 1, ATT_HEAD_DIM), lambda g, i: (g, 0, 0)),
            pl.BlockSpec((ATT_BLK, 2 * ATT_BLK), lambda g, i: (0, 0)),
        ],
        out_specs=pl.BlockSpec((ATT_BLK, ATT_HPS * ATT_HEAD_DIM), lambda g, i: (i, g)),
        out_shape=jax.ShapeDtypeStruct((s, ATT_WIDTH), BF16),
        scratch_shapes=(
            [pltpu.VMEM((ATT_BLK, ATT_HEAD_DIM), F32) for _ in range(ATT_HPS)]
            + [pltpu.VMEM((ATT_BLK, ATT_BLK), F32) for _ in range(ATT_HPS)]
        ),
        compiler_params=_cparams(("parallel", "arbitrary")),
        name="sb_attn",
    )(qkv, qkv, qkv, attn_norm_w.reshape(ATT_HEADS, 1, ATT_HEAD_DIM), uo)


def _ssd_kernel(z_ref, xs_ref, bc_ref, dt_ref, cw_ref, cb_ref, dtb_ref, alog_ref, dskip_ref, nw_ref,
                tri_ref, eh_ref, et_ref, o_ref,
                xbuf, bbuf, xc, bcc, acol, acst, dtf, decf, expf, state):
    c = pl.program_id(0)
    t = SSD_T
    w2 = SSM_WIDTH

    @pl.when(c == 0)
    def _():
        xbuf[0:SUBLANES, :] = jnp.zeros((SUBLANES, w2), F32)
        bbuf[0:SUBLANES, :] = jnp.zeros((SUBLANES, w2), F32)
        state[...] = jnp.zeros_like(state)

    def conv_chunk(ci, carry):
        for src_ref, buf, dst, woff in ((xs_ref, xbuf, xc, 0), (bc_ref, bbuf, bcc, w2)):
            cols = pl.ds(pl.multiple_of(ci * SSD_CCH, SSD_CCH), SSD_CCH)
            wcols = pl.ds(pl.multiple_of(woff + ci * SSD_CCH, SSD_CCH), SSD_CCH)
            cur = src_ref[:, cols]
            buf[SUBLANES:SUBLANES + t, cols] = cur
            acc = cur * cw_ref[CONV_WIDTH - 1:CONV_WIDTH, wcols] + cb_ref[:, wcols]
            for k in range(CONV_WIDTH - 1):
                start = SUBLANES - (CONV_WIDTH - 1) + k
                acc = acc + buf[start:start + t, cols] * cw_ref[k:k + 1, wcols]
            dst[:, cols] = _silu(acc)
            buf[0:SUBLANES, cols] = cur[t - SUBLANES:, :]
        return carry

    lax.fori_loop(0, w2 // SSD_CCH, conv_chunk, 0)

    dt = _softplus(dt_ref[...] + dtb_ref[...])
    a = -jnp.exp(alog_ref[...])
    a_cs = _dot_exact01_lhs(tri_ref[...], dt * a)
    a_last = a_cs[t - 1:t, :]
    terms = []
    for v in (dt, dt * jnp.exp(a_last - a_cs), jnp.exp(a_cs)):
        terms.extend(_split2(v))
    ex = _dot(jnp.concatenate(terms, axis=0), eh_ref[...])
    dtf[...] = ex[0:t] + ex[t:2 * t]
    decf[...] = ex[2 * t:3 * t] + ex[3 * t:4 * t]
    expf[...] = ex[4 * t:5 * t] + ex[5 * t:6 * t]
    ac = _dot(jnp.concatenate(_split3(a_cs), axis=0), et_ref[...])
    acol[...] = ac[0:t] + ac[t:2 * t] + ac[2 * t:3 * t]
    acst[...] = a_cs.T

    row = lax.broadcasted_iota(I32, (t, t), 0)
    col = lax.broadcasted_iota(I32, (t, t), 1)
    lower = col <= row
    head_of_lane = lax.broadcasted_iota(I32, (t, SSM_GROUP_WIDTH), 1) // SSM_HEAD_DIM

    def group(g, carry):
        go = pl.multiple_of(g * SSM_GROUP_WIDTH, SSM_GROUP_WIDTH)
        no = pl.multiple_of(g * SSM_STATE, SSM_STATE)
        bg = bcc[:, pl.ds(no, SSM_STATE)]
        cg = bcc[:, pl.ds(SSM_GROUPS * SSM_STATE + no, SSM_STATE)]
        cgb = cg.astype(BF16)
        cb = _dot_nt(cgb, bg.astype(BF16))
        xg = xc[:, pl.ds(go, SSM_GROUP_WIDTH)]
        xd = xg * dtf[:, pl.ds(go, SSM_GROUP_WIDTH)]
        xdec = (xg * decf[:, pl.ds(go, SSM_GROUP_WIDTH)]).astype(BF16)
        st = state[g]
        y = _dot(cgb, st.astype(BF16)) * expf[:, pl.ds(go, SSM_GROUP_WIDTH)]
        for r in range(SSM_GROUP_WIDTH // SSM_HEAD_DIM):
            h = g * (SSM_GROUP_WIDTH // SSM_HEAD_DIM) + r
            ho = pl.multiple_of(h * t, t)
            seg = acol[:, pl.ds(ho, t)] - acst[pl.ds(h, 1), :]
            lmat = jnp.exp(jnp.where(lower, seg, NEG_BIG))
            gm = (cb * lmat).astype(BF16)
            xd_r = jnp.where(head_of_lane == r, xd, 0.0).astype(BF16)
            y = y + _dot(gm, xd_r)
        state[g] = st * expf[t - 1:t, pl.ds(go, SSM_GROUP_WIDTH)] + _dot(bg.T.astype(BF16), xdec)
        y = y + dskip_ref[:, pl.ds(go, SSM_GROUP_WIDTH)] * xg
        y = y * _silu(z_ref[:, pl.ds(go, SSM_GROUP_WIDTH)])
        y = y * lax.rsqrt(jnp.mean(y * y, axis=-1, keepdims=True) + EPS) * nw_ref[:, pl.ds(go, SSM_GROUP_WIDTH)]
        o_ref[:, pl.ds(go, SSM_GROUP_WIDTH)] = y.astype(o_ref.dtype)
        return carry

    lax.fori_loop(0, SSM_GROUPS, group, 0)


def _dot_exact01_lhs(m, x):
    h1, h2, h3 = _split3(x)
    return _dot(m, h1) + _dot(m, h2) + _dot(m, h3)


def _ssd(zxbc, dt_raw, conv_w, conv_b, dt_bias, a_log, d_skip, ssd_norm_w):
    s = zxbc.shape[0]
    t = SSD_T
    w2 = SSM_WIDTH
    pad = LANES - SSM_HEADS
    tri = (jnp.arange(t)[:, None] >= jnp.arange(t)[None, :]).astype(BF16)
    heads = jnp.arange(LANES)[:, None]
    eh = (heads == (jnp.arange(w2)[None, :] // SSM_HEAD_DIM)).astype(BF16)
    et = (heads == (jnp.arange(SSM_HEADS * t)[None, :] // t)).astype(BF16)
    dtb = jnp.pad(dt_bias.astype(F32), (0, pad)).reshape(1, LANES)
    alog = jnp.pad(a_log.astype(F32), (0, pad)).reshape(1, LANES)
    dskip = jnp.repeat(d_skip.astype(F32), SSM_HEAD_DIM).reshape(1, w2)
    const = lambda c: (0, 0)
    return pl.pallas_call(
        _ssd_kernel,
        grid=(s // t,),
        in_specs=[
            pl.BlockSpec((t, w2), lambda c: (c, 0)),
            pl.BlockSpec((t, w2), lambda c: (c, 1)),
            pl.BlockSpec((t, w2), lambda c: (c, 2)),
            pl.BlockSpec((t, LANES), lambda c: (c, 0)),
            pl.BlockSpec((CONV_WIDTH, 2 * w2), const),
            pl.BlockSpec((1, 2 * w2), const),
            pl.BlockSpec((1, LANES), const),
            pl.BlockSpec((1, LANES), const),
            pl.BlockSpec((1, w2), const),
            pl.BlockSpec((1, w2), const),
            pl.BlockSpec((t, t), const),
            pl.BlockSpec((LANES, w2), const),
            pl.BlockSpec((LANES, SSM_HEADS * t), const),
        ],
        out_specs=pl.BlockSpec((t, w2), lambda c: (c, 0)),
        out_shape=jax.ShapeDtypeStruct((s, w2), BF16),
        scratch_shapes=[
            pltpu.VMEM((t + SUBLANES, w2), F32),
            pltpu.VMEM((t + SUBLANES, w2), F32),
            pltpu.VMEM((t, w2), F32),
            pltpu.VMEM((t, w2), F32),
            pltpu.VMEM((t, SSM_HEADS * t), F32),
            pltpu.VMEM((LANES, t), F32),
            pltpu.VMEM((t, w2), F32),
            pltpu.VMEM((t, w2), F32),
            pltpu.VMEM((t, w2), F32),
            pltpu.VMEM((SSM_GROUPS, SSM_STATE, SSM_GROUP_WIDTH), F32),
        ],
        compiler_params=_cparams(("arbitrary",)),
        name="ssd",
    )(zxbc, zxbc, zxbc, dt_raw, conv_w.astype(F32), conv_b.astype(F32).reshape(1, 2 * w2), dtb, alog, dskip,
      ssd_norm_w.astype(F32).reshape(1, w2), tri, eh, et)


def _out_proj_kernel(att_ref, y_ref, w_ref, x_ref, o_ref):
    acc = _dot(att_ref[...], w_ref[0:ATT_WIDTH, :])
    acc = acc + _dot(y_ref[...], w_ref[ATT_WIDTH:, :])
    o_ref[...] = x_ref[...] + acc


def _out_proj(att, y, w_b, x2):
    s, d = x2.shape
    tm, tn = 512, MM_TN
    return pl.pallas_call(
        _out_proj_kernel,
        grid=(d // tn, s // tm),
        in_specs=[
            pl.BlockSpec((tm, ATT_WIDTH), lambda j, i: (i, 0)),
            pl.BlockSpec((tm, SSM_WIDTH), lambda j, i: (i, 0)),
            pl.BlockSpec((ATT_WIDTH + SSM_WIDTH, tn), lambda j, i: (0, j)),
            pl.BlockSpec((tm, tn), lambda j, i: (i, j)),
        ],
        out_specs=pl.BlockSpec((tm, tn), lambda j, i: (i, j)),
        out_shape=jax.ShapeDtypeStruct((s, d), F32),
        compiler_params=_cparams(("parallel", "parallel")),
        name="out_proj",
    )(att, y, w_b, x2)


def _router_kernel(h_ref, w_ref, whi_ref, wlo_ref, b_ref, up_ref, route_ref):
    x = h_ref[...]
    u = x * lax.rsqrt(jnp.mean(x * x, axis=-1, keepdims=True) + EPS) * w_ref[...]
    half = D_MODEL // 2
    up_ref[...] = _pack_bf16_pair(u[:, :half], u[:, half:])
    uh, ul = _split2(u)
    whi = whi_ref[...]
    logits = _dot(uh, whi) + _dot(ul, whi) + _dot(uh, wlo_ref[...]) + b_ref[...]
    lane = lax.broadcasted_iota(I32, logits.shape, 1).astype(F32)
    ninf = -jnp.inf
    nolane = float(LANES)
    gl = jnp.where(lane < MOE_GROUPS, logits, ninf)
    gmax = jnp.max(gl, axis=-1, keepdims=True)
    gsum = jnp.sum(jnp.exp(gl - gmax), axis=-1, keepdims=True)
    g_w = 1.0 / gsum
    g_idx = jnp.min(jnp.where(gl == gmax, lane, nolane), axis=-1, keepdims=True)
    lo_lane = MOE_GROUPS + g_idx * EXPERTS_PER_GROUP
    el = jnp.where((lane >= lo_lane) & (lane < lo_lane + EXPERTS_PER_GROUP), logits, ninf)
    m0 = jnp.max(el, axis=-1, keepdims=True)
    i0 = jnp.min(jnp.where(el == m0, lane, nolane), axis=-1, keepdims=True)
    el2 = jnp.where(lane == i0, ninf, el)
    m1 = jnp.max(el2, axis=-1, keepdims=True)
    i1 = jnp.min(jnp.where(el2 == m1, lane, nolane), axis=-1, keepdims=True)
    tt = jnp.exp(m1 - m0)
    w0 = g_w / (1.0 + tt)
    w1 = g_w * tt / (1.0 + tt)
    e0 = i0 - MOE_GROUPS
    e1 = i1 - MOE_GROUPS
    out = jnp.where(lane == 0, e0, jnp.where(lane == 1, e1, jnp.where(lane == 2, w0, jnp.where(lane == 3, w1, 0.0))))
    route_ref[...] = out


def _router(h1, norm_w, wr_hi, wr_lo, br):
    s, d = h1.shape
    return pl.pallas_call(
        _router_kernel,
        grid=(s // NORM_ROWS,),
        in_specs=[
            pl.BlockSpec((NORM_ROWS, d), lambda i: (i, 0)),
            pl.BlockSpec((1, d), lambda i: (0, 0)),
            pl.BlockSpec((d, LANES), lambda i: (0, 0)),
            pl.BlockSpec((d, LANES), lambda i: (0, 0)),
            pl.BlockSpec((1, LANES), lambda i: (0, 0)),
        ],
        out_specs=[
            pl.BlockSpec((NORM_ROWS, d // 2), lambda i: (i, 0)),
            pl.BlockSpec((NORM_ROWS, LANES), lambda i: (i, 0)),
        ],
        out_shape=[jax.ShapeDtypeStruct((s, d // 2), U32), jax.ShapeDtypeStruct((s, LANES), F32)],
        compiler_params=_cparams(("parallel",)),
        name="router",
    )(h1, norm_w.reshape(1, d), wr_hi, wr_lo, br)


def _gather_rows_kernel(idx_ref, nvalid_ref, src_ref, o_ref, sem):
    base = pl.program_id(0) * GATHER_BLK

    def copy(r):
        return pltpu.make_async_copy(src_ref.at[pl.ds(idx_ref[base + r], 1)], o_ref.at[pl.ds(r, 1)], sem)

    @pl.when(base < nvalid_ref[0])
    def _():
        def issue(g, carry):
            for k in range(GATHER_UNROLL):
                copy(g * GATHER_UNROLL + k).start(priority=k % 2)
            return carry

        lax.fori_loop(0, GATHER_BLK // GATHER_UNROLL, issue, 0)

        def drain(r, carry):
            copy(r).wait()
            return carry

        lax.fori_loop(0, GATHER_BLK, drain, 0, unroll=GATHER_UNROLL)

    @pl.when(base >= nvalid_ref[0])
    def _():
        o_ref[...] = jnp.zeros_like(o_ref)


def _gather_rows(idx, nvalid, src, n_out, name):
    return pl.pallas_call(
        _gather_rows_kernel,
        grid_spec=pltpu.PrefetchScalarGridSpec(
            num_scalar_prefetch=2,
            grid=(n_out // GATHER_BLK,),
            in_specs=[pl.BlockSpec(memory_space=pl.ANY)],
            out_specs=pl.BlockSpec((GATHER_BLK, src.shape[1]), lambda b, idx_ref, nv_ref: (b, 0)),
            scratch_shapes=[pltpu.SemaphoreType.DMA(())],
        ),
        out_shape=jax.ShapeDtypeStruct((n_out, src.shape[1]), src.dtype),
        compiler_params=_cparams(("arbitrary",)),
        name=name,
    )(idx, nvalid, src)


def _zero_fill_blocks(zbuf, dst_hbm, first_blk, sem):
    zbuf[...] = jnp.zeros_like(zbuf)

    def copy(b):
        r0 = pl.multiple_of(b * MOE_BLK, MOE_BLK)
        return pltpu.make_async_copy(zbuf, dst_hbm.at[pl.ds(r0, MOE_BLK)], sem)

    def start(b, carry):
        copy(b).start()
        return carry

    def wait(b, carry):
        copy(b).wait()
        return carry

    lax.fori_loop(first_blk, MOE_NBLK, start, 0)
    lax.fori_loop(first_blk, MOE_NBLK, wait, 0)


def _walk_blocks(n, n_next, is_first, is_last, in_copy, next_first_copy, out_copy, compute, flags):
    @pl.when(is_first)
    def _():
        flags[0] = 0
        flags[1] = 0
        flags[2] = 0

    @pl.when((n > 0) & (flags[2] == 0))
    def _():
        in_copy(0, 0).start(priority=BLOCK_DMA_PRIORITY)

    def body(j, carry):
        slot = j % 2
        in_copy(j, slot).wait()

        @pl.when(j + 1 < n)
        def _():
            in_copy(j + 1, 1 - slot).start(priority=BLOCK_DMA_PRIORITY)

        @pl.when(flags[slot] == 1)
        def _():
            out_copy(j, slot).wait()

        compute(slot)
        out_copy(j, slot).start()
        flags[slot] = 1
        return carry

    lax.fori_loop(0, n, body, 0)

    start_next = jnp.logical_and(jnp.logical_not(is_last), n_next > 0)

    @pl.when(start_next)
    def _():
        next_first_copy().start(priority=BLOCK_DMA_PRIORITY)

    flags[2] = start_next.astype(I32)

    @pl.when(is_last)
    def _():
        for slot in range(2):
            @pl.when(flags[slot] == 1)
            def _():
                out_copy(0, slot).wait()


def _moe_up_kernel(first_ref, cnt_ref, nact_ref, xs_hbm, wg_ref, wu_ref, hdn_hbm,
                   wgb, wub, xbuf, hbuf, zbuf, in_sem, out_sem, z_sem, flags):
    e = pl.program_id(0)
    ht = pl.program_id(1)
    nht = pl.num_programs(1)
    n = cnt_ref[e]
    b0 = first_ref[e]
    col0 = pl.multiple_of(ht * MOE_TH, LANES)
    e_next = jnp.minimum(jnp.where(ht + 1 < nht, e, e + 1), N_EXPERTS - 1)
    is_last = (e == N_EXPERTS - 1) & (ht == nht - 1)

    def block_rows(blk):
        return pl.ds(pl.multiple_of(blk * MOE_BLK, MOE_BLK), MOE_BLK)

    def rows(j):
        return block_rows(b0 + j)

    def in_copy(j, slot):
        return pltpu.make_async_copy(xs_hbm.at[rows(j)], xbuf.at[slot], in_sem.at[slot])

    def next_first_copy():
        return pltpu.make_async_copy(xs_hbm.at[block_rows(first_ref[e_next])], xbuf.at[0], in_sem.at[0])

    def out_copy(j, slot):
        return pltpu.make_async_copy(hbuf.at[slot], hdn_hbm.at[rows(j), pl.ds(col0, MOE_TH)], out_sem.at[slot])

    @pl.when((e == 0) & (ht == 0))
    def _():
        _zero_fill_blocks(zbuf, hdn_hbm, nact_ref[0], z_sem)

    @pl.when(n > 0)
    def _():
        wgb[...] = wg_ref[0].astype(BF16)
        wub[...] = wu_ref[0].astype(BF16)

    def compute(slot):
        xa, xb = _unpack_bf16_pair(xbuf[slot])
        x = jnp.concatenate([xa.astype(BF16), xb.astype(BF16)], axis=-1)
        g = _dot(x, wgb[...])
        u = _dot(x, wub[...])
        hbuf[slot] = (_silu(g) * u).astype(BF16)

    _walk_blocks(n, cnt_ref[e_next], (e == 0) & (ht == 0), is_last, in_copy, next_first_copy, out_copy, compute, flags)


def _moe_up(blk_first, blk_cnt, nact, xs, w_gate, w_up):
    nht = EXPERT_HIDDEN // MOE_TH
    half = D_MODEL // 2
    return pl.pallas_call(
        _moe_up_kernel,
        grid_spec=pltpu.PrefetchScalarGridSpec(
            num_scalar_prefetch=3,
            grid=(N_EXPERTS, nht),
            in_specs=[
                pl.BlockSpec(memory_space=pl.ANY),
                pl.BlockSpec((1, D_MODEL, MOE_TH), lambda e, ht, bf, bc, na: (e, 0, ht)),
                pl.BlockSpec((1, D_MODEL, MOE_TH), lambda e, ht, bf, bc, na: (e, 0, ht)),
            ],
            out_specs=pl.BlockSpec(memory_space=pl.ANY),
            scratch_shapes=[
                pltpu.VMEM((D_MODEL, MOE_TH), BF16),
                pltpu.VMEM((D_MODEL, MOE_TH), BF16),
                pltpu.VMEM((2, MOE_BLK, half), U32),
                pltpu.VMEM((2, MOE_BLK, MOE_TH), BF16),
                pltpu.VMEM((MOE_BLK, EXPERT_HIDDEN), BF16),
                pltpu.SemaphoreType.DMA((2,)),
                pltpu.SemaphoreType.DMA((2,)),
                pltpu.SemaphoreType.DMA(()),
                pltpu.SMEM((3,), I32),
            ],
        ),
        out_shape=jax.ShapeDtypeStruct((MOE_ROWS, EXPERT_HIDDEN), BF16),
        compiler_params=_cparams(("arbitrary", "arbitrary")),
        name="moe_up",
    )(blk_first, blk_cnt, nact, xs, w_gate, w_up)


def _moe_down_kernel(first_ref, cnt_ref, nact_ref, hdn_hbm, wd_ref, ys_hbm,
                     wdb, hbuf, ybuf, zbuf, in_sem, out_sem, z_sem, flags):
    e = pl.program_id(0)
    n = cnt_ref[e]
    b0 = first_ref[e]
    half = D_MODEL // 2
    e_next = jnp.minimum(e + 1, N_EXPERTS - 1)

    def block_rows(blk):
        return pl.ds(pl.multiple_of(blk * MOE_BLK, MOE_BLK), MOE_BLK)

    def rows(j):
        return block_rows(b0 + j)

    def in_copy(j, slot):
        return pltpu.make_async_copy(hdn_hbm.at[rows(j)], hbuf.at[slot], in_sem.at[slot])

    def next_first_copy():
        return pltpu.make_async_copy(hdn_hbm.at[block_rows(first_ref[e_next])], hbuf.at[0], in_sem.at[0])

    def out_copy(j, slot):
        return pltpu.make_async_copy(ybuf.at[slot], ys_hbm.at[rows(j)], out_sem.at[slot])

    @pl.when(e == 0)
    def _():
        _zero_fill_blocks(zbuf, ys_hbm, nact_ref[0], z_sem)

    @pl.when(n > 0)
    def _():
        wdb[...] = wd_ref[0].astype(BF16)

    def compute(slot):
        y = _dot(hbuf[slot], wdb[...])
        ybuf[slot] = _pack_bf16_pair(y[:, :half], y[:, half:])

    _walk_blocks(n, cnt_ref[e_next], e == 0, e == N_EXPERTS - 1, in_copy, next_first_copy, out_copy, compute, flags)


def _moe_down(blk_first, blk_cnt, nact, hdn, w_down):
    half = D_MODEL // 2
    return pl.pallas_call(
        _moe_down_kernel,
        grid_spec=pltpu.PrefetchScalarGridSpec(
            num_scalar_prefetch=3,
            grid=(N_EXPERTS,),
            in_specs=[
                pl.BlockSpec(memory_space=pl.ANY),
                pl.BlockSpec((1, EXPERT_HIDDEN, D_MODEL), lambda e, bf, bc, na: (e, 0, 0)),
            ],
            out_specs=pl.BlockSpec(memory_space=pl.ANY),
            scratch_shapes=[
                pltpu.VMEM((EXPERT_HIDDEN, D_MODEL), BF16),
                pltpu.VMEM((2, MOE_BLK, EXPERT_HIDDEN), BF16),
                pltpu.VMEM((2, MOE_BLK, half), U32),
                pltpu.VMEM((MOE_BLK, half), U32),
                pltpu.SemaphoreType.DMA((2,)),
                pltpu.SemaphoreType.DMA((2,)),
                pltpu.SemaphoreType.DMA(()),
                pltpu.SMEM((3,), I32),
            ],
        ),
        out_shape=jax.ShapeDtypeStruct((MOE_ROWS, half), U32),
        compiler_params=_cparams(("arbitrary",)),
        name="moe_down",
    )(blk_first, blk_cnt, nact, hdn, w_down)


def _combine_kernel(h_ref, ya0_ref, ya1_ref, route_ref, w_ref, o_ref):
    route = route_ref[...]
    w0 = route[:, 2:3]
    w1 = route[:, 3:4]
    a0, b0 = _unpack_bf16_pair(ya0_ref[...])
    a1, b1 = _unpack_bf16_pair(ya1_ref[...])
    moe = jnp.concatenate([w0 * a0 + w1 * a1, w0 * b0 + w1 * b1], axis=-1)
    h = h_ref[...] + moe
    o_ref[...] = h * lax.rsqrt(jnp.mean(h * h, axis=-1, keepdims=True) + EPS) * w_ref[...]


def _combine(h1, ya, route, norm_w):
    s, d = h1.shape
    nb = s // NORM_ROWS
    return pl.pallas_call(
        _combine_kernel,
        grid=(nb,),
        in_specs=[
            pl.BlockSpec((NORM_ROWS, d), lambda i: (i, 0)),
            pl.BlockSpec((NORM_ROWS, d // 2), lambda i: (i, 0)),
            pl.BlockSpec((NORM_ROWS, d // 2), lambda i: (i + nb, 0)),
            pl.BlockSpec((NORM_ROWS, LANES), lambda i: (i, 0)),
            pl.BlockSpec((1, d), lambda i: (0, 0)),
        ],
        out_specs=pl.BlockSpec((NORM_ROWS, d), lambda i: (i, 0)),
        out_shape=jax.ShapeDtypeStruct((s, d), F32),
        compiler_params=_cparams(("parallel",)),
        name="combine",
    )(h1, ya, ya, route, norm_w.reshape(1, d))


def _dispatch_plan(route):
    s = route.shape[0]
    flat_e = route[:, :EXPERT_TOP_K].astype(I32).reshape(-1)
    onehot = (flat_e[:, None] == jnp.arange(N_EXPERTS, dtype=I32)[None, :]).astype(I32)
    csum = jnp.cumsum(onehot, axis=0)
    rank = jnp.sum((csum - 1) * onehot, axis=1)
    counts = csum[-1]
    padded = (counts + MOE_BLK - 1) // MOE_BLK * MOE_BLK
    pend = jnp.cumsum(padded)
    pstart = pend - padded
    dest = (pstart[flat_e] + rank).astype(I32)
    flat_t = jnp.repeat(jnp.arange(s, dtype=I32), EXPERT_TOP_K)
    row_tok = (jnp.arange(MOE_ROWS, dtype=I32) % s).at[dest].set(flat_t)
    nact = (pend[-1] // MOE_BLK).astype(I32).reshape(1)
    blk_first = (pstart // MOE_BLK).astype(I32)
    blk_cnt = (padded // MOE_BLK).astype(I32)
    return dest, row_tok, blk_first, blk_cnt, nact


def kernel(x, norm_mix_w, w_in, conv_w, conv_b, dt_bias, a_log, d_skip, ssd_norm_w, attn_norm_w, w_out,
           norm_ffn_w, w_group, b_group, w_expert, b_expert, w_gate, w_up, w_down, norm_final_w):
    b, s, d = x.shape
    x2 = x.reshape(b * s, d)
    n_qkv = 3 * ATT_WIDTH
    n_zxbc = SSM_WIDTH + SSM_WIDTH + 2 * SSM_GROUPS * SSM_STATE

    w_in_b = w_in.astype(BF16)
    w_dt = jnp.pad(w_in_b[:, n_qkv + n_zxbc:], ((0, 0), (0, LANES - SSM_HEADS)))
    w_out_b = w_out.astype(BF16)
    wr = jnp.concatenate(
        [w_group, jnp.transpose(w_expert, (1, 0, 2)).reshape(d, N_EXPERTS),
         jnp.zeros((d, LANES - MOE_GROUPS - N_EXPERTS), F32)], axis=1)
    wr_hi = wr.astype(BF16)
    wr_lo = (wr - wr_hi.astype(F32)).astype(BF16)
    br = jnp.concatenate([b_group, b_expert.reshape(-1), jnp.zeros((LANES - MOE_GROUPS - N_EXPERTS,), F32)]).reshape(1, LANES)
    kk = jnp.arange(ATT_BLK)
    uo = jnp.concatenate([(kk[:, None] > kk[None, :]).astype(BF16), jnp.ones((ATT_BLK, ATT_BLK), BF16)], axis=1)

    u, dt_raw = _norm_dt(x2, norm_mix_w, w_dt)
    qkv = _proj_heads(u, w_in_b, 0, n_qkv)
    zxbc = _proj(u, w_in_b, n_qkv, n_zxbc, F32, "proj_zxbc")
    att = _attention(qkv, attn_norm_w, uo)
    y = _ssd(zxbc, dt_raw, conv_w, conv_b, dt_bias, a_log, d_skip, ssd_norm_w)
    h1 = _out_proj(att, y, w_out_b, x2)

    up, route = _router(h1, norm_ffn_w, wr_hi, wr_lo, br)
    dest, row_tok, blk_first, blk_cnt, nact = _dispatch_plan(route)
    xs = _gather_rows(row_tok, nact * MOE_BLK, up, MOE_ROWS, "dispatch")
    hdn = _moe_up(blk_first, blk_cnt, nact, xs, w_gate, w_up)
    ys = _moe_down(blk_first, blk_cnt, nact, hdn, w_down)
    slot_major = dest.reshape(s, EXPERT_TOP_K).T.reshape(-1)
    n_slots = jnp.full((1,), s * EXPERT_TOP_K, I32)
    ya = _gather_rows(slot_major, n_slots, ys, s * EXPERT_TOP_K, "undispatch")
    out = _combine(h1, ya, route, norm_final_w)
    return out.reshape(b, s, d)
```

```python
import functools
import math

import jax
import jax.numpy as jnp
from jax import lax
from jax.experimental import pallas as pl
from jax.experimental.pallas import tpu as pltpu

F32 = jnp.float32
BF16 = jnp.bfloat16
U32 = jnp.uint32
I32 = jnp.int32

D_MODEL = 4096
SEQ = 8192
ATT_WIDTH = 2048
ATT_HEAD_DIM = 128
ATT_HEADS = 16
SSM_WIDTH = 2048
SSM_HEAD_DIM = 64
SSM_HEADS = 32
SSM_STATE = 128
SSM_GROUPS = 8
SSM_GROUP_WIDTH = SSM_WIDTH // SSM_GROUPS
CONV_WIDTH = 4
MOE_GROUPS = 8
EXPERTS_PER_GROUP = 8
N_EXPERTS = 64
EXPERT_TOP_K = 2
EXPERT_HIDDEN = 768
EPS = 1e-6

LANES = 128
SUBLANES = 8
VMEM_LIMIT = 56 * 1024 * 1024

NORM_ROWS = 256
MM_TM = 1024
MM_TN = 512
OUT_TM = 512
OUT_TN = 1024
ATT_BLK = 128
ATT_HPS = 8
SSD_T = 128
SSD_CCH = 256
MOE_BLK = 128
MOE_TH = 384
MOE_NBLK = -(-(SEQ * EXPERT_TOP_K + N_EXPERTS * (MOE_BLK - 1)) // MOE_BLK)
MOE_ROWS = MOE_NBLK * MOE_BLK
GATHER_BLK = 512
GATHER_UNROLL = 8
BLOCK_DMA_PRIORITY = 1
ATT_UNDERFLOW = 104.5

NEG_BIG = -1e30


def _cparams(sem):
    return pltpu.CompilerParams(dimension_semantics=sem, vmem_limit_bytes=VMEM_LIMIT)


def _softplus(x):
    return jnp.maximum(x, 0.0) + jnp.log(1.0 + jnp.exp(-jnp.abs(x)))


def _silu(x):
    return x / (1.0 + jnp.exp(-x))


def _split2(x):
    hi = x.astype(BF16)
    lo = (x - hi.astype(F32)).astype(BF16)
    return hi, lo


def _split3(x):
    h1 = x.astype(BF16)
    r1 = x - h1.astype(F32)
    h2 = r1.astype(BF16)
    h3 = (r1 - h2.astype(F32)).astype(BF16)
    return h1, h2, h3


def _dot(a, b):
    return jnp.dot(a, b, preferred_element_type=F32)


def _dot_nt(a, b):
    return lax.dot_general(a, b, (((1,), (1,)), ((), ())), preferred_element_type=F32)


def _dot_exact01(x, m, parts):
    terms = _split3(x) if parts == 3 else _split2(x)
    acc = _dot(terms[0], m)
    for t in terms[1:]:
        acc = acc + _dot(t, m)
    return acc


def _pack_bf16_pair(a, b):
    ab = pltpu.bitcast(a.astype(BF16).astype(F32), U32)
    bb = pltpu.bitcast(b.astype(BF16).astype(F32), U32)
    return ab | (bb >> 16)


def _unpack_bf16_pair(p):
    a = pltpu.bitcast(p & jnp.uint32(0xFFFF0000), F32)
    b = pltpu.bitcast(p << 16, F32)
    return a, b


def _norm_dt_kernel(x_ref, w_ref, wdt_ref, u_ref, dt_ref):
    x = x_ref[...]
    y = x * lax.rsqrt(jnp.mean(x * x, axis=-1, keepdims=True) + EPS) * w_ref[...]
    ub = y.astype(BF16)
    u_ref[...] = ub
    dt_ref[...] = _dot(ub, wdt_ref[...])


def _norm_dt(x2, norm_w, wdt_b):
    s, d = x2.shape
    return pl.pallas_call(
        _norm_dt_kernel,
        grid=(s // NORM_ROWS,),
        in_specs=[
            pl.BlockSpec((NORM_ROWS, d), lambda i: (i, 0)),
            pl.BlockSpec((1, d), lambda i: (0, 0)),
            pl.BlockSpec((d, LANES), lambda i: (0, 0)),
        ],
        out_specs=[
            pl.BlockSpec((NORM_ROWS, d), lambda i: (i, 0)),
            pl.BlockSpec((NORM_ROWS, LANES), lambda i: (i, 0)),
        ],
        out_shape=[jax.ShapeDtypeStruct((s, d), BF16), jax.ShapeDtypeStruct((s, LANES), F32)],
        compiler_params=_cparams(("parallel",)),
        name="norm_dt",
    )(x2, norm_w.reshape(1, d), wdt_b)


def _cast_weight_tile(w_ref, wb):
    @pl.when(pl.program_id(1) == 0)
    def _():
        wb[...] = w_ref[...].astype(BF16)


def _proj_heads_kernel(u_ref, w_ref, o_ref, wb):
    _cast_weight_tile(w_ref, wb)
    acc = _dot(u_ref[...], wb[...])
    for hh in range(MM_TN // ATT_HEAD_DIM):
        o_ref[hh] = acc[:, hh * ATT_HEAD_DIM:(hh + 1) * ATT_HEAD_DIM].astype(o_ref.dtype)


def _proj_heads(u, w, col0, n):
    s, d = u.shape
    hpt = MM_TN // ATT_HEAD_DIM
    j0 = col0 // MM_TN
    return pl.pallas_call(
        _proj_heads_kernel,
        grid=(n // MM_TN, s // MM_TM),
        in_specs=[
            pl.BlockSpec((MM_TM, d), lambda j, i: (i, 0)),
            pl.BlockSpec((d, MM_TN), lambda j, i: (0, j0 + j)),
        ],
        out_specs=pl.BlockSpec((hpt, MM_TM, ATT_HEAD_DIM), lambda j, i: (j, i, 0)),
        out_shape=jax.ShapeDtypeStruct((n // ATT_HEAD_DIM, s, ATT_HEAD_DIM), BF16),
        scratch_shapes=[pltpu.VMEM((d, MM_TN), BF16)],
        compiler_params=_cparams(("parallel", "arbitrary")),
        name="proj_qkv",
    )(u, w)


def _proj_kernel(u_ref, w_ref, o_ref, wb):
    _cast_weight_tile(w_ref, wb)
    o_ref[...] = _dot(u_ref[...], wb[...]).astype(o_ref.dtype)


def _proj(u, w, col0, n, out_dtype, name):
    s, d = u.shape
    j0 = col0 // MM_TN
    return pl.pallas_call(
        _proj_kernel,
        grid=(n // MM_TN, s // MM_TM),
        in_specs=[
            pl.BlockSpec((MM_TM, d), lambda j, i: (i, 0)),
            pl.BlockSpec((d, MM_TN), lambda j, i: (0, j0 + j)),
        ],
        out_specs=pl.BlockSpec((MM_TM, MM_TN), lambda j, i: (i, j)),
        out_shape=jax.ShapeDtypeStruct((s, n), out_dtype),
        scratch_shapes=[pltpu.VMEM((d, MM_TN), BF16)],
        compiler_params=_cparams(("parallel", "arbitrary")),
        name=name,
    )(u, w)


def _attn_kernel(q_ref, k_ref, v_ref, nw_ref, uo_ref, o_ref, *scratch):
    acc_refs, c_refs = scratch[:ATT_HPS], scratch[ATT_HPS:]
    _attn_body(q_ref, k_ref, v_ref, nw_ref, uo_ref, o_ref, acc_refs, c_refs)


def _attn_body(q_ref, k_ref, v_ref, nw_ref, uo_ref, o_ref, acc_refs, c_refs):
    i = pl.program_id(1)
    uo = uo_ref[...]
    scale = 1.0 / math.sqrt(ATT_HEAD_DIM)
    row = lax.broadcasted_iota(I32, (ATT_BLK, ATT_BLK), 0)
    col = lax.broadcasted_iota(I32, (ATT_BLK, ATT_BLK), 1)
    strict = col < row

    def block(j, masked):
        off = pl.multiple_of(j * ATT_BLK, ATT_BLK)
        heads = range(ATT_HPS)
        zs = [_dot_nt(q_ref[hh], k_ref[hh, pl.ds(off, ATT_BLK), :]) * scale for hh in heads]
        sps = [_softplus(z) for z in zs]
        log_betas = [z - sp for z, sp in zip(zs, sps)]
        log_1ms = [jnp.where(strict, -sp, 0.0) if masked else -sp for sp in sps]
        splits = [_split2(x) for x in log_1ms]
        r2s = [_dot(hi, uo) + _dot(lo, uo) for hi, lo in splits]
        cmax = None
        ws = []
        for hh in heads:
            r = r2s[hh][:, :ATT_BLK]
            tot = r2s[hh][:, ATT_BLK:]
            if masked:
                ws.append(jnp.where(strict, jnp.exp(log_betas[hh] + r), 0.0).astype(BF16))
                c_new = tot
            else:
                c_old = c_refs[hh][...]
                ws.append(jnp.exp(log_betas[hh] + r + c_old).astype(BF16))
                c_new = c_old + tot
            c_refs[hh][...] = c_new
            cmax = c_new if cmax is None else jnp.maximum(cmax, c_new)
        for hh in heads:
            pv = _dot(ws[hh], v_ref[hh, pl.ds(off, ATT_BLK), :])
            if masked:
                acc_refs[hh][...] = pv
            else:
                acc_refs[hh][...] += pv
        return jnp.max(cmax)

    def cond(carry):
        return (carry[0] < i) & (carry[1] > -ATT_UNDERFLOW)

    def body(carry):
        t = carry[0]
        return t + 1, block(i - 1 - t, False)

    lax.while_loop(cond, body, (jnp.int32(0), block(i, True)))
    for hh in range(ATT_HPS):
        acc = acc_refs[hh][...]
        y = acc * lax.rsqrt(jnp.mean(acc * acc, axis=-1, keepdims=True) + EPS) * nw_ref[hh]
        o_ref[:, hh * ATT_HEAD_DIM:(hh + 1) * ATT_HEAD_DIM] = y.astype(o_ref.dtype)


def _attention(qkv, attn_norm_w, uo):
    s = qkv.shape[1]
    ng = ATT_HEADS // ATT_HPS
    return pl.pallas_call(
        _attn_kernel,
        grid=(ng, s // ATT_BLK),
        in_specs=[
            pl.BlockSpec((ATT_HPS, ATT_BLK, ATT_HEAD_DIM), lambda g, i: (g, i, 0)),
            pl.BlockSpec((ATT_HPS, s, ATT_HEAD_DIM), lambda g, i: (ng + g, 0, 0), pipeline_mode=pl.Buffered(1)),
            pl.BlockSpec((ATT_HPS, s, ATT_HEAD_DIM), lambda g, i: (2 * ng + g, 0, 0), pipeline_mode=pl.Buffered(1)),
            pl.BlockSpec((ATT_HPS, 1, ATT_HEAD_DIM), lambda g, i: (g, 0, 0)),
            pl.BlockSpec((ATT_BLK, 2 * ATT_BLK), lambda g, i: (0, 0)),
        ],
        out_specs=pl.BlockSpec((ATT_BLK, ATT_HPS * ATT_HEAD_DIM), lambda g, i: (i, g)),
        out_shape=jax.ShapeDtypeStruct((s, ATT_WIDTH), BF16),
        scratch_shapes=(
            [pltpu.VMEM((ATT_BLK, ATT_HEAD_DIM), F32) for _ in range(ATT_HPS)]
            + [pltpu.VMEM((ATT_BLK, ATT_BLK), F32) for _ in range(ATT_HPS)]
        ),
        compiler_params=_cparams(("parallel", "arbitrary")),
        name="sb_attn",
    )(qkv, qkv, qkv, attn_norm_w.reshape(ATT_HEADS, 1, ATT_HEAD_DIM), uo)


def _ssd_kernel(z_ref, xs_ref, bc_ref, dt_ref, cw_ref, cb_ref, dtb_ref, alog_ref, dskip_ref, nw_ref,
                tri_ref, eh_ref, et_ref, o_ref,
                xbuf, bbuf, xc, bcc, acol, acst, dtf, decf, expf, state):
    c = pl.program_id(0)
    t = SSD_T
    w2 = SSM_WIDTH

    @pl.when(c == 0)
    def _():
        xbuf[0:SUBLANES, :] = jnp.zeros((SUBLANES, w2), F32)
        bbuf[0:SUBLANES, :] = jnp.zeros((SUBLANES, w2), F32)
        state[...] = jnp.zeros_like(state)

    def conv_chunk(ci, carry):
        for src_ref, buf, dst, woff in ((xs_ref, xbuf, xc, 0), (bc_ref, bbuf, bcc, w2)):
            cols = pl.ds(pl.multiple_of(ci * SSD_CCH, SSD_CCH), SSD_CCH)
            wcols = pl.ds(pl.multiple_of(woff + ci * SSD_CCH, SSD_CCH), SSD_CCH)
            cur = src_ref[:, cols]
            buf[SUBLANES:SUBLANES + t, cols] = cur
            acc = cur * cw_ref[CONV_WIDTH - 1:CONV_WIDTH, wcols] + cb_ref[:, wcols]
            for k in range(CONV_WIDTH - 1):
                start = SUBLANES - (CONV_WIDTH - 1) + k
                acc = acc + buf[start:start + t, cols] * cw_ref[k:k + 1, wcols]
            dst[:, cols] = _silu(acc)
            buf[0:SUBLANES, cols] = cur[t - SUBLANES:, :]
        return carry

    lax.fori_loop(0, w2 // SSD_CCH, conv_chunk, 0)

    dt = _softplus(dt_ref[...] + dtb_ref[...])
    a = -jnp.exp(alog_ref[...])
    a_cs = _dot_exact01_lhs(tri_ref[...], dt * a)
    a_last = a_cs[t - 1:t, :]
    terms = []
    for v in (dt, dt * jnp.exp(a_last - a_cs), jnp.exp(a_cs)):
        terms.extend(_split2(v))
    ex = _dot(jnp.concatenate(terms, axis=0), eh_ref[...])
    dtf[...] = ex[0:t] + ex[t:2 * t]
    decf[...] = ex[2 * t:3 * t] + ex[3 * t:4 * t]
    expf[...] = ex[4 * t:5 * t] + ex[5 * t:6 * t]
    ac = _dot(jnp.concatenate(_split3(a_cs), axis=0), et_ref[...])
    acol[...] = ac[0:t] + ac[t:2 * t] + ac[2 * t:3 * t]
    acst[...] = a_cs.T

    row = lax.broadcasted_iota(I32, (t, t), 0)
    col = lax.broadcasted_iota(I32, (t, t), 1)
    lower = col <= row
    head_of_lane = lax.broadcasted_iota(I32, (t, SSM_GROUP_WIDTH), 1) // SSM_HEAD_DIM

    def group(g, carry):
        go = pl.multiple_of(g * SSM_GROUP_WIDTH, SSM_GROUP_WIDTH)
        no = pl.multiple_of(g * SSM_STATE, SSM_STATE)
        bg = bcc[:, pl.ds(no, SSM_STATE)]
        cg = bcc[:, pl.ds(SSM_GROUPS * SSM_STATE + no, SSM_STATE)]
        cgb = cg.astype(BF16)
        cb = _dot_nt(cgb, bg.astype(BF16))
        xg = xc[:, pl.ds(go, SSM_GROUP_WIDTH)]
        xd = xg * dtf[:, pl.ds(go, SSM_GROUP_WIDTH)]
        xdec = (xg * decf[:, pl.ds(go, SSM_GROUP_WIDTH)]).astype(BF16)
        st = state[g]
        y = _dot(cgb, st.astype(BF16)) * expf[:, pl.ds(go, SSM_GROUP_WIDTH)]
        for r in range(SSM_GROUP_WIDTH // SSM_HEAD_DIM):
            h = g * (SSM_GROUP_WIDTH // SSM_HEAD_DIM) + r
            ho = pl.multiple_of(h * t, t)
            seg = acol[:, pl.ds(ho, t)] - acst[pl.ds(h, 1), :]
            lmat = jnp.exp(jnp.where(lower, seg, NEG_BIG))
            gm = (cb * lmat).astype(BF16)
            xd_r = jnp.where(head_of_lane == r, xd, 0.0).astype(BF16)
            y = y + _dot(gm, xd_r)
        state[g] = st * expf[t - 1:t, pl.ds(go, SSM_GROUP_WIDTH)] + _dot(bg.T.astype(BF16), xdec)
        y = y + dskip_ref[:, pl.ds(go, SSM_GROUP_WIDTH)] * xg
        y = y * _silu(z_ref[:, pl.ds(go, SSM_GROUP_WIDTH)])
        y = y * lax.rsqrt(jnp.mean(y * y, axis=-1, keepdims=True) + EPS) * nw_ref[:, pl.ds(go, SSM_GROUP_WIDTH)]
        o_ref[:, pl.ds(go, SSM_GROUP_WIDTH)] = y.astype(o_ref.dtype)
        return carry

    lax.fori_loop(0, SSM_GROUPS, group, 0)


def _dot_exact01_lhs(m, x):
    h1, h2, h3 = _split3(x)
    return _dot(m, h1) + _dot(m, h2) + _dot(m, h3)


def _ssd(zxbc, dt_raw, conv_w, conv_b, dt_bias, a_log, d_skip, ssd_norm_w):
    s = zxbc.shape[0]
    t = SSD_T
    w2 = SSM_WIDTH
    pad = LANES - SSM_HEADS
    tri = (jnp.arange(t)[:, None] >= jnp.arange(t)[None, :]).astype(BF16)
    heads = jnp.arange(LANES)[:, None]
    eh = (heads == (jnp.arange(w2)[None, :] // SSM_HEAD_DIM)).astype(BF16)
    et = (heads == (jnp.arange(SSM_HEADS * t)[None, :] // t)).astype(BF16)
    dtb = jnp.pad(dt_bias.astype(F32), (0, pad)).reshape(1, LANES)
    alog = jnp.pad(a_log.astype(F32), (0, pad)).reshape(1, LANES)
    dskip = jnp.repeat(d_skip.astype(F32), SSM_HEAD_DIM).reshape(1, w2)
    const = lambda c: (0, 0)
    return pl.pallas_call(
        _ssd_kernel,
        grid=(s // t,),
        in_specs=[
            pl.BlockSpec((t, w2), lambda c: (c, 0)),
            pl.BlockSpec((t, w2), lambda c: (c, 1)),
            pl.BlockSpec((t, w2), lambda c: (c, 2)),
            pl.BlockSpec((t, LANES), lambda c: (c, 0)),
            pl.BlockSpec((CONV_WIDTH, 2 * w2), const),
            pl.BlockSpec((1, 2 * w2), const),
            pl.BlockSpec((1, LANES), const),
            pl.BlockSpec((1, LANES), const),
            pl.BlockSpec((1, w2), const),
            pl.BlockSpec((1, w2), const),
            pl.BlockSpec((t, t), const),
            pl.BlockSpec((LANES, w2), const),
            pl.BlockSpec((LANES, SSM_HEADS * t), const),
        ],
        out_specs=pl.BlockSpec((t, w2), lambda c: (c, 0)),
        out_shape=jax.ShapeDtypeStruct((s, w2), BF16),
        scratch_shapes=[
            pltpu.VMEM((t + SUBLANES, w2), F32),
            pltpu.VMEM((t + SUBLANES, w2), F32),
            pltpu.VMEM((t, w2), F32),
            pltpu.VMEM((t, w2), F32),
            pltpu.VMEM((t, SSM_HEADS * t), F32),
            pltpu.VMEM((LANES, t), F32),
            pltpu.VMEM((t, w2), F32),
            pltpu.VMEM((t, w2), F32),
            pltpu.VMEM((t, w2), F32),
            pltpu.VMEM((SSM_GROUPS, SSM_STATE, SSM_GROUP_WIDTH), F32),
        ],
        compiler_params=_cparams(("arbitrary",)),
        name="ssd",
    )(zxbc, zxbc, zxbc, dt_raw, conv_w.astype(F32), conv_b.astype(F32).reshape(1, 2 * w2), dtb, alog, dskip,
      ssd_norm_w.astype(F32).reshape(1, w2), tri, eh, et)


def _out_proj_kernel(att_ref, y_ref, w_ref, x_ref, o_ref):
    acc = _dot(att_ref[...], w_ref[0:ATT_WIDTH, :])
    acc = acc + _dot(y_ref[...], w_ref[ATT_WIDTH:, :])
    o_ref[...] = x_ref[...] + acc


def _out_proj(att, y, w_b, x2):
    s, d = x2.shape
    tm, tn = OUT_TM, OUT_TN
    return pl.pallas_call(
        _out_proj_kernel,
        grid=(d // tn, s // tm),
        in_specs=[
            pl.BlockSpec((tm, ATT_WIDTH), lambda j, i: (i, 0)),
            pl.BlockSpec((tm, SSM_WIDTH), lambda j, i: (i, 0)),
            pl.BlockSpec((ATT_WIDTH + SSM_WIDTH, tn), lambda j, i: (0, j)),
            pl.BlockSpec((tm, tn), lambda j, i: (i, j)),
        ],
        out_specs=pl.BlockSpec((tm, tn), lambda j, i: (i, j)),
        out_shape=jax.ShapeDtypeStruct((s, d), F32),
        compiler_params=_cparams(("parallel", "parallel")),
        name="out_proj",
    )(att, y, w_b, x2)


def _router_kernel(h_ref, w_ref, whi_ref, wlo_ref, b_ref, up_ref, route_ref):
    x = h_ref[...]
    u = x * lax.rsqrt(jnp.mean(x * x, axis=-1, keepdims=True) + EPS) * w_ref[...]
    half = D_MODEL // 2
    up_ref[...] = _pack_bf16_pair(u[:, :half], u[:, half:])
    uh, ul = _split2(u)
    whi = whi_ref[...]
    logits = _dot(uh, whi) + _dot(ul, whi) + _dot(uh, wlo_ref[...]) + b_ref[...]
    lane = lax.broadcasted_iota(I32, logits.shape, 1).astype(F32)
    ninf = -jnp.inf
    nolane = float(LANES)
    gl = jnp.where(lane < MOE_GROUPS, logits, ninf)
    gmax = jnp.max(gl, axis=-1, keepdims=True)
    gsum = jnp.sum(jnp.exp(gl - gmax), axis=-1, keepdims=True)
    g_w = 1.0 / gsum
    g_idx = jnp.min(jnp.where(gl == gmax, lane, nolane), axis=-1, keepdims=True)
    lo_lane = MOE_GROUPS + g_idx * EXPERTS_PER_GROUP
    el = jnp.where((lane >= lo_lane) & (lane < lo_lane + EXPERTS_PER_GROUP), logits, ninf)
    m0 = jnp.max(el, axis=-1, keepdims=True)
    i0 = jnp.min(jnp.where(el == m0, lane, nolane), axis=-1, keepdims=True)
    el2 = jnp.where(lane == i0, ninf, el)
    m1 = jnp.max(el2, axis=-1, keepdims=True)
    i1 = jnp.min(jnp.where(el2 == m1, lane, nolane), axis=-1, keepdims=True)
    tt = jnp.exp(m1 - m0)
    w0 = g_w / (1.0 + tt)
    w1 = g_w * tt / (1.0 + tt)
    e0 = i0 - MOE_GROUPS
    e1 = i1 - MOE_GROUPS
    out = jnp.where(lane == 0, e0, jnp.where(lane == 1, e1, jnp.where(lane == 2, w0, jnp.where(lane == 3, w1, 0.0))))
    route_ref[...] = out


def _router(h1, norm_w, wr_hi, wr_lo, br):
    s, d = h1.shape
    return pl.pallas_call(
        _router_kernel,
        grid=(s // NORM_ROWS,),
        in_specs=[
            pl.BlockSpec((NORM_ROWS, d), lambda i: (i, 0)),
            pl.BlockSpec((1, d), lambda i: (0, 0)),
            pl.BlockSpec((d, LANES), lambda i: (0, 0)),
            pl.BlockSpec((d, LANES), lambda i: (0, 0)),
            pl.BlockSpec((1, LANES), lambda i: (0, 0)),
        ],
        out_specs=[
            pl.BlockSpec((NORM_ROWS, d // 2), lambda i: (i, 0)),
            pl.BlockSpec((NORM_ROWS, LANES), lambda i: (i, 0)),
        ],
        out_shape=[jax.ShapeDtypeStruct((s, d // 2), U32), jax.ShapeDtypeStruct((s, LANES), F32)],
        compiler_params=_cparams(("parallel",)),
        name="router",
    )(h1, norm_w.reshape(1, d), wr_hi, wr_lo, br)


def _gather_rows_kernel(idx_ref, nvalid_ref, src_ref, o_ref, sem):
    base = pl.program_id(0) * GATHER_BLK

    def copy(r):
        return pltpu.make_async_copy(src_ref.at[pl.ds(idx_ref[base + r], 1)], o_ref.at[pl.ds(r, 1)], sem)

    @pl.when(base < nvalid_ref[0])
    def _():
        def issue(g, carry):
            for k in range(GATHER_UNROLL):
                copy(g * GATHER_UNROLL + k).start(priority=k % 2)
            return carry

        lax.fori_loop(0, GATHER_BLK // GATHER_UNROLL, issue, 0)

        def drain(r, carry):
            copy(r).wait()
            return carry

        lax.fori_loop(0, GATHER_BLK, drain, 0, unroll=GATHER_UNROLL)

    @pl.when(base >= nvalid_ref[0])
    def _():
        o_ref[...] = jnp.zeros_like(o_ref)


def _gather_rows(idx, nvalid, src, n_out, name):
    return pl.pallas_call(
        _gather_rows_kernel,
        grid_spec=pltpu.PrefetchScalarGridSpec(
            num_scalar_prefetch=2,
            grid=(n_out // GATHER_BLK,),
            in_specs=[pl.BlockSpec(memory_space=pl.ANY)],
            out_specs=pl.BlockSpec((GATHER_BLK, src.shape[1]), lambda b, idx_ref, nv_ref: (b, 0)),
            scratch_shapes=[pltpu.SemaphoreType.DMA(())],
        ),
        out_shape=jax.ShapeDtypeStruct((n_out, src.shape[1]), src.dtype),
        compiler_params=_cparams(("arbitrary",)),
        name=name,
    )(idx, nvalid, src)


def _zero_fill_blocks(zbuf, dst_hbm, first_blk, sem):
    zbuf[...] = jnp.zeros_like(zbuf)

    def copy(b):
        r0 = pl.multiple_of(b * MOE_BLK, MOE_BLK)
        return pltpu.make_async_copy(zbuf, dst_hbm.at[pl.ds(r0, MOE_BLK)], sem)

    def start(b, carry):
        copy(b).start()
        return carry

    def wait(b, carry):
        copy(b).wait()
        return carry

    lax.fori_loop(first_blk, MOE_NBLK, start, 0)
    lax.fori_loop(first_blk, MOE_NBLK, wait, 0)


def _walk_blocks(n, n_next, is_first, is_last, in_copy, next_first_copy, out_copy, compute, flags):
    @pl.when(is_first)
    def _():
        flags[0] = 0
        flags[1] = 0
        flags[2] = 0

    @pl.when((n > 0) & (flags[2] == 0))
    def _():
        in_copy(0, 0).start(priority=BLOCK_DMA_PRIORITY)

    def body(j, carry):
        slot = j % 2
        in_copy(j, slot).wait()

        @pl.when(j + 1 < n)
        def _():
            in_copy(j + 1, 1 - slot).start(priority=BLOCK_DMA_PRIORITY)

        @pl.when(flags[slot] == 1)
        def _():
            out_copy(j, slot).wait()

        compute(slot)
        out_copy(j, slot).start()
        flags[slot] = 1
        return carry

    lax.fori_loop(0, n, body, 0)

    start_next = jnp.logical_and(jnp.logical_not(is_last), n_next > 0)

    @pl.when(start_next)
    def _():
        next_first_copy().start(priority=BLOCK_DMA_PRIORITY)

    flags[2] = start_next.astype(I32)

    @pl.when(is_last)
    def _():
        for slot in range(2):
            @pl.when(flags[slot] == 1)
            def _():
                out_copy(0, slot).wait()


def _moe_up_kernel(first_ref, cnt_ref, nact_ref, xs_hbm, wg_ref, wu_ref, hdn_hbm,
                   wgb, wub, xbuf, hbuf, zbuf, in_sem, out_sem, z_sem, flags):
    e = pl.program_id(0)
    ht = pl.program_id(1)
    nht = pl.num_programs(1)
    n = cnt_ref[e]
    b0 = first_ref[e]
    col0 = pl.multiple_of(ht * MOE_TH, LANES)
    e_next = jnp.minimum(jnp.where(ht + 1 < nht, e, e + 1), N_EXPERTS - 1)
    is_last = (e == N_EXPERTS - 1) & (ht == nht - 1)

    def block_rows(blk):
        return pl.ds(pl.multiple_of(blk * MOE_BLK, MOE_BLK), MOE_BLK)

    def rows(j):
        return block_rows(b0 + j)

    def in_copy(j, slot):
        return pltpu.make_async_copy(xs_hbm.at[rows(j)], xbuf.at[slot], in_sem.at[slot])

    def next_first_copy():
        return pltpu.make_async_copy(xs_hbm.at[block_rows(first_ref[e_next])], xbuf.at[0], in_sem.at[0])

    def out_copy(j, slot):
        return pltpu.make_async_copy(hbuf.at[slot], hdn_hbm.at[rows(j), pl.ds(col0, MOE_TH)], out_sem.at[slot])

    @pl.when((e == 0) & (ht == 0))
    def _():
        _zero_fill_blocks(zbuf, hdn_hbm, nact_ref[0], z_sem)

    @pl.when(n > 0)
    def _():
        wgb[...] = wg_ref[0].astype(BF16)
        wub[...] = wu_ref[0].astype(BF16)

    def compute(slot):
        xa, xb = _unpack_bf16_pair(xbuf[slot])
        x = jnp.concatenate([xa.astype(BF16), xb.astype(BF16)], axis=-1)
        g = _dot(x, wgb[...])
        u = _dot(x, wub[...])
        hbuf[slot] = (_silu(g) * u).astype(BF16)

    _walk_blocks(n, cnt_ref[e_next], (e == 0) & (ht == 0), is_last, in_copy, next_first_copy, out_copy, compute, flags)


def _moe_up(blk_first, blk_cnt, nact, xs, w_gate, w_up):
    nht = EXPERT_HIDDEN // MOE_TH
    half = D_MODEL // 2
    return pl.pallas_call(
        _moe_up_kernel,
        grid_spec=pltpu.PrefetchScalarGridSpec(
            num_scalar_prefetch=3,
            grid=(N_EXPERTS, nht),
            in_specs=[
                pl.BlockSpec(memory_space=pl.ANY),
                pl.BlockSpec((1, D_MODEL, MOE_TH), lambda e, ht, bf, bc, na: (e, 0, ht)),
                pl.BlockSpec((1, D_MODEL, MOE_TH), lambda e, ht, bf, bc, na: (e, 0, ht)),
            ],
            out_specs=pl.BlockSpec(memory_space=pl.ANY),
            scratch_shapes=[
                pltpu.VMEM((D_MODEL, MOE_TH), BF16),
                pltpu.VMEM((D_MODEL, MOE_TH), BF16),
                pltpu.VMEM((2, MOE_BLK, half), U32),
                pltpu.VMEM((2, MOE_BLK, MOE_TH), BF16),
                pltpu.VMEM((MOE_BLK, EXPERT_HIDDEN), BF16),
                pltpu.SemaphoreType.DMA((2,)),
                pltpu.SemaphoreType.DMA((2,)),
                pltpu.SemaphoreType.DMA(()),
                pltpu.SMEM((3,), I32),
            ],
        ),
        out_shape=jax.ShapeDtypeStruct((MOE_ROWS, EXPERT_HIDDEN), BF16),
        compiler_params=_cparams(("arbitrary", "arbitrary")),
        name="moe_up",
    )(blk_first, blk_cnt, nact, xs, w_gate, w_up)


def _moe_down_kernel(first_ref, cnt_ref, nact_ref, hdn_hbm, wd_ref, ys_hbm,
                     wdb, hbuf, ybuf, zbuf, in_sem, out_sem, z_sem, flags):
    e = pl.program_id(0)
    n = cnt_ref[e]
    b0 = first_ref[e]
    half = D_MODEL // 2
    e_next = jnp.minimum(e + 1, N_EXPERTS - 1)

    def block_rows(blk):
        return pl.ds(pl.multiple_of(blk * MOE_BLK, MOE_BLK), MOE_BLK)

    def rows(j):
        return block_rows(b0 + j)

    def in_copy(j, slot):
        return pltpu.make_async_copy(hdn_hbm.at[rows(j)], hbuf.at[slot], in_sem.at[slot])

    def next_first_copy():
        return pltpu.make_async_copy(hdn_hbm.at[block_rows(first_ref[e_next])], hbuf.at[0], in_sem.at[0])

    def out_copy(j, slot):
        return pltpu.make_async_copy(ybuf.at[slot], ys_hbm.at[rows(j)], out_sem.at[slot])

    @pl.when(e == 0)
    def _():
        _zero_fill_blocks(zbuf, ys_hbm, nact_ref[0], z_sem)

    @pl.when(n > 0)
    def _():
        wdb[...] = wd_ref[0].astype(BF16)

    def compute(slot):
        y = _dot(hbuf[slot], wdb[...])
        ybuf[slot] = _pack_bf16_pair(y[:, :half], y[:, half:])

    _walk_blocks(n, cnt_ref[e_next], e == 0, e == N_EXPERTS - 1, in_copy, next_first_copy, out_copy, compute, flags)


def _moe_down(blk_first, blk_cnt, nact, hdn, w_down):
    half = D_MODEL // 2
    return pl.pallas_call(
        _moe_down_kernel,
        grid_spec=pltpu.PrefetchScalarGridSpec(
            num_scalar_prefetch=3,
            grid=(N_EXPERTS,),
            in_specs=[
                pl.BlockSpec(memory_space=pl.ANY),
                pl.BlockSpec((1, EXPERT_HIDDEN, D_MODEL), lambda e, bf, bc, na: (e, 0, 0)),
            ],
            out_specs=pl.BlockSpec(memory_space=pl.ANY),
            scratch_shapes=[
                pltpu.VMEM((EXPERT_HIDDEN, D_MODEL), BF16),
                pltpu.VMEM((2, MOE_BLK, EXPERT_HIDDEN), BF16),
                pltpu.VMEM((2, MOE_BLK, half), U32),
                pltpu.VMEM((MOE_BLK, half), U32),
                pltpu.SemaphoreType.DMA((2,)),
                pltpu.SemaphoreType.DMA((2,)),
                pltpu.SemaphoreType.DMA(()),
                pltpu.SMEM((3,), I32),
            ],
        ),
        out_shape=jax.ShapeDtypeStruct((MOE_ROWS, half), U32),
        compiler_params=_cparams(("arbitrary",)),
        name="moe_down",
    )(blk_first, blk_cnt, nact, hdn, w_down)


def _combine_kernel(h_ref, ya0_ref, ya1_ref, route_ref, w_ref, o_ref):
    route = route_ref[...]
    w0 = route[:, 2:3]
    w1 = route[:, 3:4]
    a0, b0 = _unpack_bf16_pair(ya0_ref[...])
    a1, b1 = _unpack_bf16_pair(ya1_ref[...])
    moe = jnp.concatenate([w0 * a0 + w1 * a1, w0 * b0 + w1 * b1], axis=-1)
    h = h_ref[...] + moe
    o_ref[...] = h * lax.rsqrt(jnp.mean(h * h, axis=-1, keepdims=True) + EPS) * w_ref[...]


def _combine(h1, ya, route, norm_w):
    s, d = h1.shape
    nb = s // NORM_ROWS
    return pl.pallas_call(
        _combine_kernel,
        grid=(nb,),
        in_specs=[
            pl.BlockSpec((NORM_ROWS, d), lambda i: (i, 0)),
            pl.BlockSpec((NORM_ROWS, d // 2), lambda i: (i, 0)),
            pl.BlockSpec((NORM_ROWS, d // 2), lambda i: (i + nb, 0)),
            pl.BlockSpec((NORM_ROWS, LANES), lambda i: (i, 0)),
            pl.BlockSpec((1, d), lambda i: (0, 0)),
        ],
        out_specs=pl.BlockSpec((NORM_ROWS, d), lambda i: (i, 0)),
        out_shape=jax.ShapeDtypeStruct((s, d), F32),
        compiler_params=_cparams(("parallel",)),
        name="combine",
    )(h1, ya, ya, route, norm_w.reshape(1, d))


def _dispatch_plan(route):
    s = route.shape[0]
    flat_e = route[:, :EXPERT_TOP_K].astype(I32).reshape(-1)
    onehot = (flat_e[:, None] == jnp.arange(N_EXPERTS, dtype=I32)[None, :]).astype(I32)
    csum = jnp.cumsum(onehot, axis=0)
    rank = jnp.sum((csum - 1) * onehot, axis=1)
    counts = csum[-1]
    padded = (counts + MOE_BLK - 1) // MOE_BLK * MOE_BLK
    pend = jnp.cumsum(padded)
    pstart = pend - padded
    dest = (pstart[flat_e] + rank).astype(I32)
    flat_t = jnp.repeat(jnp.arange(s, dtype=I32), EXPERT_TOP_K)
    row_tok = (jnp.arange(MOE_ROWS, dtype=I32) % s).at[dest].set(flat_t)
    nact = (pend[-1] // MOE_BLK).astype(I32).reshape(1)
    blk_first = (pstart // MOE_BLK).astype(I32)
    blk_cnt = (padded // MOE_BLK).astype(I32)
    return dest, row_tok, blk_first, blk_cnt, nact


def kernel(x, norm_mix_w, w_in, conv_w, conv_b, dt_bias, a_log, d_skip, ssd_norm_w, attn_norm_w, w_out,
           norm_ffn_w, w_group, b_group, w_expert, b_expert, w_gate, w_up, w_down, norm_final_w):
    b, s, d = x.shape
    x2 = x.reshape(b * s, d)
    n_qkv = 3 * ATT_WIDTH
    n_zxbc = SSM_WIDTH + SSM_WIDTH + 2 * SSM_GROUPS * SSM_STATE

    w_dt = jnp.pad(w_in[:, n_qkv + n_zxbc:].astype(BF16), ((0, 0), (0, LANES - SSM_HEADS)))
    w_out_b = w_out.astype(BF16)
    wr = jnp.concatenate(
        [w_group, jnp.transpose(w_expert, (1, 0, 2)).reshape(d, N_EXPERTS),
         jnp.zeros((d, LANES - MOE_GROUPS - N_EXPERTS), F32)], axis=1)
    wr_hi = wr.astype(BF16)
    wr_lo = (wr - wr_hi.astype(F32)).astype(BF16)
    br = jnp.concatenate([b_group, b_expert.reshape(-1), jnp.zeros((LANES - MOE_GROUPS - N_EXPERTS,), F32)]).reshape(1, LANES)
    kk = jnp.arange(ATT_BLK)
    uo = jnp.concatenate([(kk[:, None] > kk[None, :]).astype(BF16), jnp.ones((ATT_BLK, ATT_BLK), BF16)], axis=1)

    u, dt_raw = _norm_dt(x2, norm_mix_w, w_dt)
    qkv = _proj_heads(u, w_in, 0, n_qkv)
    zxbc = _proj(u, w_in, n_qkv, n_zxbc, F32, "proj_zxbc")
    att = _attention(qkv, attn_norm_w, uo)
    y = _ssd(zxbc, dt_raw, conv_w, conv_b, dt_bias, a_log, d_skip, ssd_norm_w)
    h1 = _out_proj(att, y, w_out_b, x2)

    up, route = _router(h1, norm_ffn_w, wr_hi, wr_lo, br)
    dest, row_tok, blk_first, blk_cnt, nact = _dispatch_plan(route)
    xs = _gather_rows(row_tok, nact * MOE_BLK, up, MOE_ROWS, "dispatch")
    hdn = _moe_up(blk_first, blk_cnt, nact, xs, w_gate, w_up)
    ys = _moe_down(blk_first, blk_cnt, nact, hdn, w_down)
    slot_major = dest.reshape(s, EXPERT_TOP_K).T.reshape(-1)
    n_slots = jnp.full((1,), s * EXPERT_TOP_K, I32)
    ya = _gather_rows(slot_major, n_slots, ys, s * EXPERT_TOP_K, "undispatch")
    out = _combine(h1, ya, route, norm_final_w)
    return out.reshape(b, s, d)
```

```python
import functools
import math

import jax
import jax.numpy as jnp
from jax import lax
from jax.experimental import pallas as pl
from jax.experimental.pallas import tpu as pltpu

F32 = jnp.float32
BF16 = jnp.bfloat16
U32 = jnp.uint32
I32 = jnp.int32

D_MODEL = 4096
SEQ = 8192
ATT_WIDTH = 2048
ATT_HEAD_DIM = 128
ATT_HEADS = 16
SSM_WIDTH = 2048
SSM_HEAD_DIM = 64
SSM_HEADS = 32
SSM_STATE = 128
SSM_GROUPS = 8
SSM_GROUP_WIDTH = SSM_WIDTH // SSM_GROUPS
CONV_WIDTH = 4
MOE_GROUPS = 8
EXPERTS_PER_GROUP = 8
N_EXPERTS = 64
EXPERT_TOP_K = 2
EXPERT_HIDDEN = 768
EPS = 1e-6

LANES = 128
SUBLANES = 8
VMEM_LIMIT = 56 * 1024 * 1024

NORM_ROWS = 256
MM_TM = 1024
MM_TN = 1024
OUT_TM = 512
OUT_TN = 1024
ATT_BLK = 128
ATT_HPS = 8
SSD_T = 128
SSD_CCH = 256
MOE_BLK = 128
MOE_TH = 384
MOE_NBLK = -(-(SEQ * EXPERT_TOP_K + N_EXPERTS * (MOE_BLK - 1)) // MOE_BLK)
MOE_ROWS = MOE_NBLK * MOE_BLK
GATHER_BLK = 512
GATHER_UNROLL = 8
BLOCK_DMA_PRIORITY = 1
ATT_UNDERFLOW = 104.5

NEG_BIG = -1e30


def _cparams(sem):
    return pltpu.CompilerParams(dimension_semantics=sem, vmem_limit_bytes=VMEM_LIMIT)


def _softplus(x):
    return jnp.maximum(x, 0.0) + jnp.log(1.0 + jnp.exp(-jnp.abs(x)))


def _silu(x):
    return x / (1.0 + jnp.exp(-x))


def _split2(x):
    hi = x.astype(BF16)
    lo = (x - hi.astype(F32)).astype(BF16)
    return hi, lo


def _split3(x):
    h1 = x.astype(BF16)
    r1 = x - h1.astype(F32)
    h2 = r1.astype(BF16)
    h3 = (r1 - h2.astype(F32)).astype(BF16)
    return h1, h2, h3


def _dot(a, b):
    return jnp.dot(a, b, preferred_element_type=F32)


def _dot_nt(a, b):
    return lax.dot_general(a, b, (((1,), (1,)), ((), ())), preferred_element_type=F32)


def _dot_exact01(x, m, parts):
    terms = _split3(x) if parts == 3 else _split2(x)
    acc = _dot(terms[0], m)
    for t in terms[1:]:
        acc = acc + _dot(t, m)
    return acc


def _pack_bf16_pair(a, b):
    ab = pltpu.bitcast(a.astype(BF16).astype(F32), U32)
    bb = pltpu.bitcast(b.astype(BF16).astype(F32), U32)
    return ab | (bb >> 16)


def _unpack_bf16_pair(p):
    a = pltpu.bitcast(p & jnp.uint32(0xFFFF0000), F32)
    b = pltpu.bitcast(p << 16, F32)
    return a, b


def _norm_dt_kernel(x_ref, w_ref, wdt_ref, u_ref, dt_ref):
    x = x_ref[...]
    y = x * lax.rsqrt(jnp.mean(x * x, axis=-1, keepdims=True) + EPS) * w_ref[...]
    ub = y.astype(BF16)
    u_ref[...] = ub
    dt_ref[...] = _dot(ub, wdt_ref[...])


def _norm_dt(x2, norm_w, wdt_b):
    s, d = x2.shape
    return pl.pallas_call(
        _norm_dt_kernel,
        grid=(s // NORM_ROWS,),
        in_specs=[
            pl.BlockSpec((NORM_ROWS, d), lambda i: (i, 0)),
            pl.BlockSpec((1, d), lambda i: (0, 0)),
            pl.BlockSpec((d, LANES), lambda i: (0, 0)),
        ],
        out_specs=[
            pl.BlockSpec((NORM_ROWS, d), lambda i: (i, 0)),
            pl.BlockSpec((NORM_ROWS, LANES), lambda i: (i, 0)),
        ],
        out_shape=[jax.ShapeDtypeStruct((s, d), BF16), jax.ShapeDtypeStruct((s, LANES), F32)],
        compiler_params=_cparams(("parallel",)),
        name="norm_dt",
    )(x2, norm_w.reshape(1, d), wdt_b)


def _proj_heads_kernel(u_ref, w_ref, o_ref):
    acc = _dot(u_ref[...], w_ref[...])
    for hh in range(MM_TN // ATT_HEAD_DIM):
        o_ref[hh] = acc[:, hh * ATT_HEAD_DIM:(hh + 1) * ATT_HEAD_DIM].astype(o_ref.dtype)


def _proj_heads(u, w, col0, n):
    s, d = u.shape
    hpt = MM_TN // ATT_HEAD_DIM
    j0 = col0 // MM_TN
    return pl.pallas_call(
        _proj_heads_kernel,
        grid=(n // MM_TN, s // MM_TM),
        in_specs=[
            pl.BlockSpec((MM_TM, d), lambda j, i: (i, 0)),
            pl.BlockSpec((d, MM_TN), lambda j, i: (0, j0 + j)),
        ],
        out_specs=pl.BlockSpec((hpt, MM_TM, ATT_HEAD_DIM), lambda j, i: (j, i, 0)),
        out_shape=jax.ShapeDtypeStruct((n // ATT_HEAD_DIM, s, ATT_HEAD_DIM), BF16),
        compiler_params=_cparams(("parallel", "parallel")),
        name="proj_qkv",
    )(u, w)


def _proj_kernel(u_ref, w_ref, o_ref):
    o_ref[...] = _dot(u_ref[...], w_ref[...]).astype(o_ref.dtype)


def _proj(u, w, col0, n, out_dtype, name):
    s, d = u.shape
    j0 = col0 // MM_TN
    return pl.pallas_call(
        _proj_kernel,
        grid=(n // MM_TN, s // MM_TM),
        in_specs=[
            pl.BlockSpec((MM_TM, d), lambda j, i: (i, 0)),
            pl.BlockSpec((d, MM_TN), lambda j, i: (0, j0 + j)),
        ],
        out_specs=pl.BlockSpec((MM_TM, MM_TN), lambda j, i: (i, j)),
        out_shape=jax.ShapeDtypeStruct((s, n), out_dtype),
        compiler_params=_cparams(("parallel", "parallel")),
        name=name,
    )(u, w)


def _attn_kernel(q_ref, k_ref, v_ref, nw_ref, uo_ref, o_ref, *scratch):
    acc_refs, c_refs = scratch[:ATT_HPS], scratch[ATT_HPS:]
    _attn_body(q_ref, k_ref, v_ref, nw_ref, uo_ref, o_ref, acc_refs, c_refs)


def _attn_body(q_ref, k_ref, v_ref, nw_ref, uo_ref, o_ref, acc_refs, c_refs):
    i = pl.program_id(1)
    uo = uo_ref[...]
    scale = 1.0 / math.sqrt(ATT_HEAD_DIM)
    row = lax.broadcasted_iota(I32, (ATT_BLK, ATT_BLK), 0)
    col = lax.broadcasted_iota(I32, (ATT_BLK, ATT_BLK), 1)
    strict = col < row

    def block(j, masked):
        off = pl.multiple_of(j * ATT_BLK, ATT_BLK)
        heads = range(ATT_HPS)
        zs = [_dot_nt(q_ref[hh], k_ref[hh, pl.ds(off, ATT_BLK), :]) * scale for hh in heads]
        sps = [_softplus(z) for z in zs]
        log_betas = [z - sp for z, sp in zip(zs, sps)]
        log_1ms = [jnp.where(strict, -sp, 0.0) if masked else -sp for sp in sps]
        splits = [_split2(x) for x in log_1ms]
        r2s = [_dot(hi, uo) + _dot(lo, uo) for hi, lo in splits]
        cmax = None
        ws = []
        for hh in heads:
            r = r2s[hh][:, :ATT_BLK]
            tot = r2s[hh][:, ATT_BLK:]
            if masked:
                ws.append(jnp.where(strict, jnp.exp(log_betas[hh] + r), 0.0).astype(BF16))
                c_new = tot
            else:
                c_old = c_refs[hh][...]
                ws.append(jnp.exp(log_betas[hh] + r + c_old).astype(BF16))
                c_new = c_old + tot
            c_refs[hh][...] = c_new
            cmax = c_new if cmax is None else jnp.maximum(cmax, c_new)
        for hh in heads:
            pv = _dot(ws[hh], v_ref[hh, pl.ds(off, ATT_BLK), :])
            if masked:
                acc_refs[hh][...] = pv
            else:
                acc_refs[hh][...] += pv
        return jnp.max(cmax)

    def cond(carry):
        return (carry[0] < i) & (carry[1] > -ATT_UNDERFLOW)

    def body(carry):
        t = carry[0]
        return t + 1, block(i - 1 - t, False)

    lax.while_loop(cond, body, (jnp.int32(0), block(i, True)))
    for hh in range(ATT_HPS):
        acc = acc_refs[hh][...]
        y = acc * lax.rsqrt(jnp.mean(acc * acc, axis=-1, keepdims=True) + EPS) * nw_ref[hh]
        o_ref[:, hh * ATT_HEAD_DIM:(hh + 1) * ATT_HEAD_DIM] = y.astype(o_ref.dtype)


def _attention(qkv, attn_norm_w, uo):
    s = qkv.shape[1]
    ng = ATT_HEADS // ATT_HPS
    return pl.pallas_call(
        _attn_kernel,
        grid=(ng, s // ATT_BLK),
        in_specs=[
            pl.BlockSpec((ATT_HPS, ATT_BLK, ATT_HEAD_DIM), lambda g, i: (g, i, 0)),
            pl.BlockSpec((ATT_HPS, s, ATT_HEAD_DIM), lambda g, i: (ng + g, 0, 0), pipeline_mode=pl.Buffered(1)),
            pl.BlockSpec((ATT_HPS, s, ATT_HEAD_DIM), lambda g, i: (2 * ng + g, 0, 0), pipeline_mode=pl.Buffered(1)),
            pl.BlockSpec((ATT_HPS, 1, ATT_HEAD_DIM), lambda g, i: (g, 0, 0)),
            pl.BlockSpec((ATT_BLK, 2 * ATT_BLK), lambda g, i: (0, 0)),
        ],
        out_specs=pl.BlockSpec((ATT_BLK, ATT_HPS * ATT_HEAD_DIM), lambda g, i: (i, g)),
        out_shape=jax.ShapeDtypeStruct((s, ATT_WIDTH), BF16),
        scratch_shapes=(
            [pltpu.VMEM((ATT_BLK, ATT_HEAD_DIM), F32) for _ in range(ATT_HPS)]
            + [pltpu.VMEM((ATT_BLK, ATT_BLK), F32) for _ in range(ATT_HPS)]
        ),
        compiler_params=_cparams(("parallel", "arbitrary")),
        name="sb_attn",
    )(qkv, qkv, qkv, attn_norm_w.reshape(ATT_HEADS, 1, ATT_HEAD_DIM), uo)


def _ssd_kernel(z_ref, xs_ref, bc_ref, dt_ref, cw_ref, cb_ref, dtb_ref, alog_ref, dskip_ref, nw_ref,
                tri_ref, eh_ref, et_ref, o_ref,
                xbuf, bbuf, xc, bcc, acol, acst, dtf, decf, expf, state):
    c = pl.program_id(0)
    t = SSD_T
    w2 = SSM_WIDTH

    @pl.when(c == 0)
    def _():
        xbuf[0:SUBLANES, :] = jnp.zeros((SUBLANES, w2), F32)
        bbuf[0:SUBLANES, :] = jnp.zeros((SUBLANES, w2), F32)
        state[...] = jnp.zeros_like(state)

    def conv_chunk(ci, carry):
        for src_ref, buf, dst, woff in ((xs_ref, xbuf, xc, 0), (bc_ref, bbuf, bcc, w2)):
            cols = pl.ds(pl.multiple_of(ci * SSD_CCH, SSD_CCH), SSD_CCH)
            wcols = pl.ds(pl.multiple_of(woff + ci * SSD_CCH, SSD_CCH), SSD_CCH)
            cur = src_ref[:, cols]
            buf[SUBLANES:SUBLANES + t, cols] = cur
            acc = cur * cw_ref[CONV_WIDTH - 1:CONV_WIDTH, wcols] + cb_ref[:, wcols]
            for k in range(CONV_WIDTH - 1):
                start = SUBLANES - (CONV_WIDTH - 1) + k
                acc = acc + buf[start:start + t, cols] * cw_ref[k:k + 1, wcols]
            dst[:, cols] = _silu(acc)
            buf[0:SUBLANES, cols] = cur[t - SUBLANES:, :]
        return carry

    lax.fori_loop(0, w2 // SSD_CCH, conv_chunk, 0)

    dt = _softplus(dt_ref[...] + dtb_ref[...])
    a = -jnp.exp(alog_ref[...])
    a_cs = _dot_exact01_lhs(tri_ref[...], dt * a)
    a_last = a_cs[t - 1:t, :]
    terms = []
    for v in (dt, dt * jnp.exp(a_last - a_cs), jnp.exp(a_cs)):
        terms.extend(_split2(v))
    ex = _dot(jnp.concatenate(terms, axis=0), eh_ref[...])
    dtf[...] = ex[0:t] + ex[t:2 * t]
    decf[...] = ex[2 * t:3 * t] + ex[3 * t:4 * t]
    expf[...] = ex[4 * t:5 * t] + ex[5 * t:6 * t]
    ac = _dot(jnp.concatenate(_split3(a_cs), axis=0), et_ref[...])
    acol[...] = ac[0:t] + ac[t:2 * t] + ac[2 * t:3 * t]
    acst[...] = a_cs.T

    row = lax.broadcasted_iota(I32, (t, t), 0)
    col = lax.broadcasted_iota(I32, (t, t), 1)
    lower = col <= row
    head_of_lane = lax.broadcasted_iota(I32, (t, SSM_GROUP_WIDTH), 1) // SSM_HEAD_DIM

    def group(g, carry):
        go = pl.multiple_of(g * SSM_GROUP_WIDTH, SSM_GROUP_WIDTH)
        no = pl.multiple_of(g * SSM_STATE, SSM_STATE)
        bg = bcc[:, pl.ds(no, SSM_STATE)]
        cg = bcc[:, pl.ds(SSM_GROUPS * SSM_STATE + no, SSM_STATE)]
        cgb = cg.astype(BF16)
        cb = _dot_nt(cgb, bg.astype(BF16))
        xg = xc[:, pl.ds(go, SSM_GROUP_WIDTH)]
        xd = xg * dtf[:, pl.ds(go, SSM_GROUP_WIDTH)]
        xdec = (xg * decf[:, pl.ds(go, SSM_GROUP_WIDTH)]).astype(BF16)
        st = state[g]
        y = _dot(cgb, st.astype(BF16)) * expf[:, pl.ds(go, SSM_GROUP_WIDTH)]
        for r in range(SSM_GROUP_WIDTH // SSM_HEAD_DIM):
            h = g * (SSM_GROUP_WIDTH // SSM_HEAD_DIM) + r
            ho = pl.multiple_of(h * t, t)
            seg = acol[:, pl.ds(ho, t)] - acst[pl.ds(h, 1), :]
            lmat = jnp.exp(jnp.where(lower, seg, NEG_BIG))
            gm = (cb * lmat).astype(BF16)
            xd_r = jnp.where(head_of_lane == r, xd, 0.0).astype(BF16)
            y = y + _dot(gm, xd_r)
        state[g] = st * expf[t - 1:t, pl.ds(go, SSM_GROUP_WIDTH)] + _dot(bg.T.astype(BF16), xdec)
        y = y + dskip_ref[:, pl.ds(go, SSM_GROUP_WIDTH)] * xg
        y = y * _silu(z_ref[:, pl.ds(go, SSM_GROUP_WIDTH)])
        y = y * lax.rsqrt(jnp.mean(y * y, axis=-1, keepdims=True) + EPS) * nw_ref[:, pl.ds(go, SSM_GROUP_WIDTH)]
        o_ref[:, pl.ds(go, SSM_GROUP_WIDTH)] = y.astype(o_ref.dtype)
        return carry

    lax.fori_loop(0, SSM_GROUPS, group, 0, unroll=8)


def _dot_exact01_lhs(m, x):
    h1, h2, h3 = _split3(x)
    return _dot(m, h1) + _dot(m, h2) + _dot(m, h3)


def _ssd(zxbc, dt_raw, conv_w, conv_b, dt_bias, a_log, d_skip, ssd_norm_w):
    s = zxbc.shape[0]
    t = SSD_T
    w2 = SSM_WIDTH
    pad = LANES - SSM_HEADS
    tri = (jnp.arange(t)[:, None] >= jnp.arange(t)[None, :]).astype(BF16)
    heads = jnp.arange(LANES)[:, None]
    eh = (heads == (jnp.arange(w2)[None, :] // SSM_HEAD_DIM)).astype(BF16)
    et = (heads == (jnp.arange(SSM_HEADS * t)[None, :] // t)).astype(BF16)
    dtb = jnp.pad(dt_bias.astype(F32), (0, pad)).reshape(1, LANES)
    alog = jnp.pad(a_log.astype(F32), (0, pad)).reshape(1, LANES)
    dskip = jnp.repeat(d_skip.astype(F32), SSM_HEAD_DIM).reshape(1, w2)
    const = lambda c: (0, 0)
    return pl.pallas_call(
        _ssd_kernel,
        grid=(s // t,),
        in_specs=[
            pl.BlockSpec((t, w2), lambda c: (c, 0)),
            pl.BlockSpec((t, w2), lambda c: (c, 1)),
            pl.BlockSpec((t, w2), lambda c: (c, 2)),
            pl.BlockSpec((t, LANES), lambda c: (c, 0)),
            pl.BlockSpec((CONV_WIDTH, 2 * w2), const),
            pl.BlockSpec((1, 2 * w2), const),
            pl.BlockSpec((1, LANES), const),
            pl.BlockSpec((1, LANES), const),
            pl.BlockSpec((1, w2), const),
            pl.BlockSpec((1, w2), const),
            pl.BlockSpec((t, t), const),
            pl.BlockSpec((LANES, w2), const),
            pl.BlockSpec((LANES, SSM_HEADS * t), const),
        ],
        out_specs=pl.BlockSpec((t, w2), lambda c: (c, 0)),
        out_shape=jax.ShapeDtypeStruct((s, w2), BF16),
        scratch_shapes=[
            pltpu.VMEM((t + SUBLANES, w2), F32),
            pltpu.VMEM((t + SUBLANES, w2), F32),
            pltpu.VMEM((t, w2), F32),
            pltpu.VMEM((t, w2), F32),
            pltpu.VMEM((t, SSM_HEADS * t), F32),
            pltpu.VMEM((LANES, t), F32),
            pltpu.VMEM((t, w2), F32),
            pltpu.VMEM((t, w2), F32),
            pltpu.VMEM((t, w2), F32),
            pltpu.VMEM((SSM_GROUPS, SSM_STATE, SSM_GROUP_WIDTH), F32),
        ],
        compiler_params=_cparams(("arbitrary",)),
        name="ssd",
    )(zxbc, zxbc, zxbc, dt_raw, conv_w.astype(F32), conv_b.astype(F32).reshape(1, 2 * w2), dtb, alog, dskip,
      ssd_norm_w.astype(F32).reshape(1, w2), tri, eh, et)


def _out_proj_kernel(att_ref, y_ref, w_ref, x_ref, o_ref):
    acc = _dot(att_ref[...], w_ref[0:ATT_WIDTH, :])
    acc = acc + _dot(y_ref[...], w_ref[ATT_WIDTH:, :])
    o_ref[...] = x_ref[...] + acc


def _out_proj(att, y, w_b, x2):
    s, d = x2.shape
    tm, tn = OUT_TM, OUT_TN
    return pl.pallas_call(
        _out_proj_kernel,
        grid=(d // tn, s // tm),
        in_specs=[
            pl.BlockSpec((tm, ATT_WIDTH), lambda j, i: (i, 0)),
            pl.BlockSpec((tm, SSM_WIDTH), lambda j, i: (i, 0)),
            pl.BlockSpec((ATT_WIDTH + SSM_WIDTH, tn), lambda j, i: (0, j)),
            pl.BlockSpec((tm, tn), lambda j, i: (i, j)),
        ],
        out_specs=pl.BlockSpec((tm, tn), lambda j, i: (i, j)),
        out_shape=jax.ShapeDtypeStruct((s, d), F32),
        compiler_params=_cparams(("parallel", "parallel")),
        name="out_proj",
    )(att, y, w_b, x2)


def _router_kernel(h_ref, w_ref, whi_ref, wlo_ref, b_ref, up_ref, route_ref):
    x = h_ref[...]
    u = x * lax.rsqrt(jnp.mean(x * x, axis=-1, keepdims=True) + EPS) * w_ref[...]
    half = D_MODEL // 2
    up_ref[...] = _pack_bf16_pair(u[:, :half], u[:, half:])
    uh, ul = _split2(u)
    whi = whi_ref[...]
    logits = _dot(uh, whi) + _dot(ul, whi) + _dot(uh, wlo_ref[...]) + b_ref[...]
    lane = lax.broadcasted_iota(I32, logits.shape, 1).astype(F32)
    ninf = -jnp.inf
    nolane = float(LANES)
    gl = jnp.where(lane < MOE_GROUPS, logits, ninf)
    gmax = jnp.max(gl, axis=-1, keepdims=True)
    gsum = jnp.sum(jnp.exp(gl - gmax), axis=-1, keepdims=True)
    g_w = 1.0 / gsum
    g_idx = jnp.min(jnp.where(gl == gmax, lane, nolane), axis=-1, keepdims=True)
    lo_lane = MOE_GROUPS + g_idx * EXPERTS_PER_GROUP
    el = jnp.where((lane >= lo_lane) & (lane < lo_lane + EXPERTS_PER_GROUP), logits, ninf)
    m0 = jnp.max(el, axis=-1, keepdims=True)
    i0 = jnp.min(jnp.where(el == m0, lane, nolane), axis=-1, keepdims=True)
    el2 = jnp.where(lane == i0, ninf, el)
    m1 = jnp.max(el2, axis=-1, keepdims=True)
    i1 = jnp.min(jnp.where(el2 == m1, lane, nolane), axis=-1, keepdims=True)
    tt = jnp.exp(m1 - m0)
    w0 = g_w / (1.0 + tt)
    w1 = g_w * tt / (1.0 + tt)
    e0 = i0 - MOE_GROUPS
    e1 = i1 - MOE_GROUPS
    out = jnp.where(lane == 0, e0, jnp.where(lane == 1, e1, jnp.where(lane == 2, w0, jnp.where(lane == 3, w1, 0.0))))
    route_ref[...] = out


def _router(h1, norm_w, wr_hi, wr_lo, br):
    s, d = h1.shape
    return pl.pallas_call(
        _router_kernel,
        grid=(s // NORM_ROWS,),
        in_specs=[
            pl.BlockSpec((NORM_ROWS, d), lambda i: (i, 0)),
            pl.BlockSpec((1, d), lambda i: (0, 0)),
            pl.BlockSpec((d, LANES), lambda i: (0, 0)),
            pl.BlockSpec((d, LANES), lambda i: (0, 0)),
            pl.BlockSpec((1, LANES), lambda i: (0, 0)),
        ],
        out_specs=[
            pl.BlockSpec((NORM_ROWS, d // 2), lambda i: (i, 0)),
            pl.BlockSpec((NORM_ROWS, LANES), lambda i: (i, 0)),
        ],
        out_shape=[jax.ShapeDtypeStruct((s, d // 2), U32), jax.ShapeDtypeStruct((s, LANES), F32)],
        compiler_params=_cparams(("parallel",)),
        name="router",
    )(h1, norm_w.reshape(1, d), wr_hi, wr_lo, br)


def _gather_rows_kernel(idx_ref, nvalid_ref, src_ref, o_ref, sem):
    base = pl.program_id(0) * GATHER_BLK

    def copy(r):
        return pltpu.make_async_copy(src_ref.at[pl.ds(idx_ref[base + r], 1)], o_ref.at[pl.ds(r, 1)], sem)

    @pl.when(base < nvalid_ref[0])
    def _():
        def issue(g, carry):
            for k in range(GATHER_UNROLL):
                copy(g * GATHER_UNROLL + k).start(priority=k % 2)
            return carry

        lax.fori_loop(0, GATHER_BLK // GATHER_UNROLL, issue, 0)

        def drain(r, carry):
            copy(r).wait()
            return carry

        lax.fori_loop(0, GATHER_BLK, drain, 0, unroll=GATHER_UNROLL)

    @pl.when(base >= nvalid_ref[0])
    def _():
        o_ref[...] = jnp.zeros_like(o_ref)


def _gather_rows(idx, nvalid, src, n_out, name):
    return pl.pallas_call(
        _gather_rows_kernel,
        grid_spec=pltpu.PrefetchScalarGridSpec(
            num_scalar_prefetch=2,
            grid=(n_out // GATHER_BLK,),
            in_specs=[pl.BlockSpec(memory_space=pl.ANY)],
            out_specs=pl.BlockSpec((GATHER_BLK, src.shape[1]), lambda b, idx_ref, nv_ref: (b, 0)),
            scratch_shapes=[pltpu.SemaphoreType.DMA(())],
        ),
        out_shape=jax.ShapeDtypeStruct((n_out, src.shape[1]), src.dtype),
        compiler_params=_cparams(("arbitrary",)),
        name=name,
    )(idx, nvalid, src)


def _zero_fill_blocks(zbuf, dst_hbm, first_blk, sem):
    zbuf[...] = jnp.zeros_like(zbuf)

    def copy(b):
        r0 = pl.multiple_of(b * MOE_BLK, MOE_BLK)
        return pltpu.make_async_copy(zbuf, dst_hbm.at[pl.ds(r0, MOE_BLK)], sem)

    def start(b, carry):
        copy(b).start()
        return carry

    def wait(b, carry):
        copy(b).wait()
        return carry

    lax.fori_loop(first_blk, MOE_NBLK, start, 0)
    lax.fori_loop(first_blk, MOE_NBLK, wait, 0)


def _walk_blocks(n, n_next, is_first, is_last, in_copy, next_first_copy, out_copy, compute, flags):
    @pl.when(is_first)
    def _():
        flags[0] = 0
        flags[1] = 0
        flags[2] = 0

    @pl.when((n > 0) & (flags[2] == 0))
    def _():
        in_copy(0, 0).start(priority=BLOCK_DMA_PRIORITY)

    def body(j, carry):
        slot = j % 2
        in_copy(j, slot).wait()

        @pl.when(j + 1 < n)
        def _():
            in_copy(j + 1, 1 - slot).start(priority=BLOCK_DMA_PRIORITY)

        @pl.when(flags[slot] == 1)
        def _():
            out_copy(j, slot).wait()

        compute(slot)
        out_copy(j, slot).start()
        flags[slot] = 1
        return carry

    lax.fori_loop(0, n, body, 0)

    start_next = jnp.logical_and(jnp.logical_not(is_last), n_next > 0)

    @pl.when(start_next)
    def _():
        next_first_copy().start(priority=BLOCK_DMA_PRIORITY)

    flags[2] = start_next.astype(I32)

    @pl.when(is_last)
    def _():
        for slot in range(2):
            @pl.when(flags[slot] == 1)
            def _():
                out_copy(0, slot).wait()


def _moe_up_kernel(first_ref, cnt_ref, nact_ref, xs_hbm, wg_ref, wu_ref, hdn_hbm,
                   wgb, wub, xbuf, hbuf, zbuf, in_sem, out_sem, z_sem, flags):
    e = pl.program_id(0)
    ht = pl.program_id(1)
    nht = pl.num_programs(1)
    n = cnt_ref[e]
    b0 = first_ref[e]
    col0 = pl.multiple_of(ht * MOE_TH, LANES)
    e_next = jnp.minimum(jnp.where(ht + 1 < nht, e, e + 1), N_EXPERTS - 1)
    is_last = (e == N_EXPERTS - 1) & (ht == nht - 1)

    def block_rows(blk):
        return pl.ds(pl.multiple_of(blk * MOE_BLK, MOE_BLK), MOE_BLK)

    def rows(j):
        return block_rows(b0 + j)

    def in_copy(j, slot):
        return pltpu.make_async_copy(xs_hbm.at[rows(j)], xbuf.at[slot], in_sem.at[slot])

    def next_first_copy():
        return pltpu.make_async_copy(xs_hbm.at[block_rows(first_ref[e_next])], xbuf.at[0], in_sem.at[0])

    def out_copy(j, slot):
        return pltpu.make_async_copy(hbuf.at[slot], hdn_hbm.at[rows(j), pl.ds(col0, MOE_TH)], out_sem.at[slot])

    @pl.when((e == 0) & (ht == 0))
    def _():
        _zero_fill_blocks(zbuf, hdn_hbm, nact_ref[0], z_sem)

    @pl.when(n > 0)
    def _():
        wgb[...] = wg_ref[0].astype(BF16)
        wub[...] = wu_ref[0].astype(BF16)

    def compute(slot):
        xa, xb = _unpack_bf16_pair(xbuf[slot])
        x = jnp.concatenate([xa.astype(BF16), xb.astype(BF16)], axis=-1)
        g = _dot(x, wgb[...])
        u = _dot(x, wub[...])
        hbuf[slot] = (_silu(g) * u).astype(BF16)

    _walk_blocks(n, cnt_ref[e_next], (e == 0) & (ht == 0), is_last, in_copy, next_first_copy, out_copy, compute, flags)


def _moe_up(blk_first, blk_cnt, nact, xs, w_gate, w_up):
    nht = EXPERT_HIDDEN // MOE_TH
    half = D_MODEL // 2
    return pl.pallas_call(
        _moe_up_kernel,
        grid_spec=pltpu.PrefetchScalarGridSpec(
            num_scalar_prefetch=3,
            grid=(N_EXPERTS, nht),
            in_specs=[
                pl.BlockSpec(memory_space=pl.ANY),
                pl.BlockSpec((1, D_MODEL, MOE_TH), lambda e, ht, bf, bc, na: (e, 0, ht)),
                pl.BlockSpec((1, D_MODEL, MOE_TH), lambda e, ht, bf, bc, na: (e, 0, ht)),
            ],
            out_specs=pl.BlockSpec(memory_space=pl.ANY),
            scratch_shapes=[
                pltpu.VMEM((D_MODEL, MOE_TH), BF16),
                pltpu.VMEM((D_MODEL, MOE_TH), BF16),
                pltpu.VMEM((2, MOE_BLK, half), U32),
                pltpu.VMEM((2, MOE_BLK, MOE_TH), BF16),
                pltpu.VMEM((MOE_BLK, EXPERT_HIDDEN), BF16),
                pltpu.SemaphoreType.DMA((2,)),
                pltpu.SemaphoreType.DMA((2,)),
                pltpu.SemaphoreType.DMA(()),
                pltpu.SMEM((3,), I32),
            ],
        ),
        out_shape=jax.ShapeDtypeStruct((MOE_ROWS, EXPERT_HIDDEN), BF16),
        compiler_params=_cparams(("arbitrary", "arbitrary")),
        name="moe_up",
    )(blk_first, blk_cnt, nact, xs, w_gate, w_up)


def _moe_down_kernel(first_ref, cnt_ref, nact_ref, hdn_hbm, wd_ref, ys_hbm,
                     wdb, hbuf, ybuf, zbuf, in_sem, out_sem, z_sem, flags):
    e = pl.program_id(0)
    n = cnt_ref[e]
    b0 = first_ref[e]
    half = D_MODEL // 2
    e_next = jnp.minimum(e + 1, N_EXPERTS - 1)

    def block_rows(blk):
        return pl.ds(pl.multiple_of(blk * MOE_BLK, MOE_BLK), MOE_BLK)

    def rows(j):
        return block_rows(b0 + j)

    def in_copy(j, slot):
        return pltpu.make_async_copy(hdn_hbm.at[rows(j)], hbuf.at[slot], in_sem.at[slot])

    def next_first_copy():
        return pltpu.make_async_copy(hdn_hbm.at[block_rows(first_ref[e_next])], hbuf.at[0], in_sem.at[0])

    def out_copy(j, slot):
        return pltpu.make_async_copy(ybuf.at[slot], ys_hbm.at[rows(j)], out_sem.at[slot])

    @pl.when(e == 0)
    def _():
        _zero_fill_blocks(zbuf, ys_hbm, nact_ref[0], z_sem)

    @pl.when(n > 0)
    def _():
        wdb[...] = wd_ref[0].astype(BF16)

    def compute(slot):
        y = _dot(hbuf[slot], wdb[...])
        ybuf[slot] = _pack_bf16_pair(y[:, :half], y[:, half:])

    _walk_blocks(n, cnt_ref[e_next], e == 0, e == N_EXPERTS - 1, in_copy, next_first_copy, out_copy, compute, flags)


def _moe_down(blk_first, blk_cnt, nact, hdn, w_down):
    half = D_MODEL // 2
    return pl.pallas_call(
        _moe_down_kernel,
        grid_spec=pltpu.PrefetchScalarGridSpec(
            num_scalar_prefetch=3,
            grid=(N_EXPERTS,),
            in_specs=[
                pl.BlockSpec(memory_space=pl.ANY),
                pl.BlockSpec((1, EXPERT_HIDDEN, D_MODEL), lambda e, bf, bc, na: (e, 0, 0)),
            ],
            out_specs=pl.BlockSpec(memory_space=pl.ANY),
            scratch_shapes=[
                pltpu.VMEM((EXPERT_HIDDEN, D_MODEL), BF16),
                pltpu.VMEM((2, MOE_BLK, EXPERT_HIDDEN), BF16),
                pltpu.VMEM((2, MOE_BLK, half), U32),
                pltpu.VMEM((MOE_BLK, half), U32),
                pltpu.SemaphoreType.DMA((2,)),
                pltpu.SemaphoreType.DMA((2,)),
                pltpu.SemaphoreType.DMA(()),
                pltpu.SMEM((3,), I32),
            ],
        ),
        out_shape=jax.ShapeDtypeStruct((MOE_ROWS, half), U32),
        compiler_params=_cparams(("arbitrary",)),
        name="moe_down",
    )(blk_first, blk_cnt, nact, hdn, w_down)


def _combine_kernel(h_ref, ya0_ref, ya1_ref, route_ref, w_ref, o_ref):
    route = route_ref[...]
    w0 = route[:, 2:3]
    w1 = route[:, 3:4]
    a0, b0 = _unpack_bf16_pair(ya0_ref[...])
    a1, b1 = _unpack_bf16_pair(ya1_ref[...])
    moe = jnp.concatenate([w0 * a0 + w1 * a1, w0 * b0 + w1 * b1], axis=-1)
    h = h_ref[...] + moe
    o_ref[...] = h * lax.rsqrt(jnp.mean(h * h, axis=-1, keepdims=True) + EPS) * w_ref[...]


def _combine(h1, ya, route, norm_w):
    s, d = h1.shape
    nb = s // NORM_ROWS
    return pl.pallas_call(
        _combine_kernel,
        grid=(nb,),
        in_specs=[
            pl.BlockSpec((NORM_ROWS, d), lambda i: (i, 0)),
            pl.BlockSpec((NORM_ROWS, d // 2), lambda i: (i, 0)),
            pl.BlockSpec((NORM_ROWS, d // 2), lambda i: (i + nb, 0)),
            pl.BlockSpec((NORM_ROWS, LANES), lambda i: (i, 0)),
            pl.BlockSpec((1, d), lambda i: (0, 0)),
        ],
        out_specs=pl.BlockSpec((NORM_ROWS, d), lambda i: (i, 0)),
        out_shape=jax.ShapeDtypeStruct((s, d), F32),
        compiler_params=_cparams(("parallel",)),
        name="combine",
    )(h1, ya, ya, route, norm_w.reshape(1, d))


def _dispatch_plan(route):
    s = route.shape[0]
    flat_e = route[:, :EXPERT_TOP_K].astype(I32).reshape(-1)
    onehot = (flat_e[:, None] == jnp.arange(N_EXPERTS, dtype=I32)[None, :]).astype(I32)
    csum = jnp.cumsum(onehot, axis=0)
    rank = jnp.sum((csum - 1) * onehot, axis=1)
    counts = csum[-1]
    padded = (counts + MOE_BLK - 1) // MOE_BLK * MOE_BLK
    pend = jnp.cumsum(padded)
    pstart = pend - padded
    dest = (pstart[flat_e] + rank).astype(I32)
    flat_t = jnp.repeat(jnp.arange(s, dtype=I32), EXPERT_TOP_K)
    row_tok = (jnp.arange(MOE_ROWS, dtype=I32) % s).at[dest].set(flat_t)
    nact = (pend[-1] // MOE_BLK).astype(I32).reshape(1)
    blk_first = (pstart // MOE_BLK).astype(I32)
    blk_cnt = (padded // MOE_BLK).astype(I32)
    return dest, row_tok, blk_first, blk_cnt, nact


def kernel(x, norm_mix_w, w_in, conv_w, conv_b, dt_bias, a_log, d_skip, ssd_norm_w, attn_norm_w, w_out,
           norm_ffn_w, w_group, b_group, w_expert, b_expert, w_gate, w_up, w_down, norm_final_w):
    b, s, d = x.shape
    x2 = x.reshape(b * s, d)
    n_qkv = 3 * ATT_WIDTH
    n_zxbc = SSM_WIDTH + SSM_WIDTH + 2 * SSM_GROUPS * SSM_STATE

    w_in_b = w_in.astype(BF16)
    w_dt = jnp.pad(w_in_b[:, n_qkv + n_zxbc:], ((0, 0), (0, LANES - SSM_HEADS)))
    w_out_b = w_out.astype(BF16)
    wr = jnp.concatenate(
        [w_group, jnp.transpose(w_expert, (1, 0, 2)).reshape(d, N_EXPERTS),
         jnp.zeros((d, LANES - MOE_GROUPS - N_EXPERTS), F32)], axis=1)
    wr_hi = wr.astype(BF16)
    wr_lo = (wr - wr_hi.astype(F32)).astype(BF16)
    br = jnp.concatenate([b_group, b_expert.reshape(-1), jnp.zeros((LANES - MOE_GROUPS - N_EXPERTS,), F32)]).reshape(1, LANES)
    kk = jnp.arange(ATT_BLK)
    uo = jnp.concatenate([(kk[:, None] > kk[None, :]).astype(BF16), jnp.ones((ATT_BLK, ATT_BLK), BF16)], axis=1)

    u, dt_raw = _norm_dt(x2, norm_mix_w, w_dt)
    qkv = _proj_heads(u, w_in_b, 0, n_qkv)
    zxbc = _proj(u, w_in_b, n_qkv, n_zxbc, F32, "proj_zxbc")
    att = _attention(qkv, attn_norm_w, uo)
    y = _ssd(zxbc, dt_raw, conv_w, conv_b, dt_bias, a_log, d_skip, ssd_norm_w)
    h1 = _out_proj(att, y, w_out_b, x2)

    up, route = _router(h1, norm_ffn_w, wr_hi, wr_lo, br)
    dest, row_tok, blk_first, blk_cnt, nact = _dispatch_plan(route)
    xs = _gather_rows(row_tok, nact * MOE_BLK, up, MOE_ROWS, "dispatch")
    hdn = _moe_up(blk_first, blk_cnt, nact, xs, w_gate, w_up)
    ys = _moe_down(blk_first, blk_cnt, nact, hdn, w_down)
    slot_major = dest.reshape(s, EXPERT_TOP_K).T.reshape(-1)
    n_slots = jnp.full((1,), s * EXPERT_TOP_K, I32)
    ya = _gather_rows(slot_major, n_slots, ys, s * EXPERT_TOP_K, "undispatch")
    out = _combine(h1, ya, route, norm_final_w)
    return out.reshape(b, s, d)
```

```python
import functools
import math

import jax
import jax.numpy as jnp
from jax import lax
from jax.experimental import pallas as pl
from jax.experimental.pallas import tpu as pltpu

F32 = jnp.float32
BF16 = jnp.bfloat16
U32 = jnp.uint32
I32 = jnp.int32

D_MODEL = 4096
SEQ = 8192
ATT_WIDTH = 2048
ATT_HEAD_DIM = 128
ATT_HEADS = 16
SSM_WIDTH = 2048
SSM_HEAD_DIM = 64
SSM_HEADS = 32
SSM_STATE = 128
SSM_GROUPS = 8
SSM_GROUP_WIDTH = SSM_WIDTH // SSM_GROUPS
CONV_WIDTH = 4
MOE_GROUPS = 8
EXPERTS_PER_GROUP = 8
N_EXPERTS = 64
EXPERT_TOP_K = 2
EXPERT_HIDDEN = 768
EPS = 1e-6

LANES = 128
SUBLANES = 8
VMEM_LIMIT = 56 * 1024 * 1024

NORM_ROWS = 256
MM_TM = 1024
MM_TN = 1024
OUT_TM = 512
OUT_TN = 1024
ATT_BLK = 128
ATT_HPS = 8
SSD_T = 128
SSD_CCH = 256
MOE_BLK = 128
MOE_TH = 384
MOE_NBLK = -(-(SEQ * EXPERT_TOP_K + N_EXPERTS * (MOE_BLK - 1)) // MOE_BLK)
MOE_ROWS = MOE_NBLK * MOE_BLK
GATHER_BLK = 512
GATHER_UNROLL = 8
BLOCK_DMA_PRIORITY = 1
ATT_UNDERFLOW = 104.5

NEG_BIG = -1e30


def _cparams(sem):
    return pltpu.CompilerParams(dimension_semantics=sem, vmem_limit_bytes=VMEM_LIMIT)


def _softplus(x):
    return jnp.maximum(x, 0.0) + jnp.log(1.0 + jnp.exp(-jnp.abs(x)))


def _silu(x):
    return x / (1.0 + jnp.exp(-x))


def _split2(x):
    hi = x.astype(BF16)
    lo = (x - hi.astype(F32)).astype(BF16)
    return hi, lo


def _split3(x):
    h1 = x.astype(BF16)
    r1 = x - h1.astype(F32)
    h2 = r1.astype(BF16)
    h3 = (r1 - h2.astype(F32)).astype(BF16)
    return h1, h2, h3


def _dot(a, b):
    return jnp.dot(a, b, preferred_element_type=F32)


def _dot_nt(a, b):
    return lax.dot_general(a, b, (((1,), (1,)), ((), ())), preferred_element_type=F32)


def _dot_exact01(x, m, parts):
    terms = _split3(x) if parts == 3 else _split2(x)
    acc = _dot(terms[0], m)
    for t in terms[1:]:
        acc = acc + _dot(t, m)
    return acc


def _pack_bf16_pair(a, b):
    ab = pltpu.bitcast(a.astype(BF16).astype(F32), U32)
    bb = pltpu.bitcast(b.astype(BF16).astype(F32), U32)
    return ab | (bb >> 16)


def _unpack_bf16_pair(p):
    a = pltpu.bitcast(p & jnp.uint32(0xFFFF0000), F32)
    b = pltpu.bitcast(p << 16, F32)
    return a, b


def _norm_dt_kernel(x_ref, w_ref, wdt_ref, u_ref, dt_ref):
    x = x_ref[...]
    y = x * lax.rsqrt(jnp.mean(x * x, axis=-1, keepdims=True) + EPS) * w_ref[...]
    ub = y.astype(BF16)
    u_ref[...] = ub
    dt_ref[...] = _dot(ub, wdt_ref[...])


def _norm_dt(x2, norm_w, wdt_b):
    s, d = x2.shape
    return pl.pallas_call(
        _norm_dt_kernel,
        grid=(s // NORM_ROWS,),
        in_specs=[
            pl.BlockSpec((NORM_ROWS, d), lambda i: (i, 0)),
            pl.BlockSpec((1, d), lambda i: (0, 0)),
            pl.BlockSpec((d, LANES), lambda i: (0, 0)),
        ],
        out_specs=[
            pl.BlockSpec((NORM_ROWS, d), lambda i: (i, 0)),
            pl.BlockSpec((NORM_ROWS, LANES), lambda i: (i, 0)),
        ],
        out_shape=[jax.ShapeDtypeStruct((s, d), BF16), jax.ShapeDtypeStruct((s, LANES), F32)],
        compiler_params=_cparams(("parallel",)),
        name="norm_dt",
    )(x2, norm_w.reshape(1, d), wdt_b)


def _proj_heads_kernel(u_ref, w_ref, o_ref):
    acc = _dot(u_ref[...], w_ref[...])
    for hh in range(MM_TN // ATT_HEAD_DIM):
        o_ref[hh] = acc[:, hh * ATT_HEAD_DIM:(hh + 1) * ATT_HEAD_DIM].astype(o_ref.dtype)


def _proj_heads(u, w, col0, n):
    s, d = u.shape
    hpt = MM_TN // ATT_HEAD_DIM
    j0 = col0 // MM_TN
    return pl.pallas_call(
        _proj_heads_kernel,
        grid=(n // MM_TN, s // MM_TM),
        in_specs=[
            pl.BlockSpec((MM_TM, d), lambda j, i: (i, 0)),
            pl.BlockSpec((d, MM_TN), lambda j, i: (0, j0 + j)),
        ],
        out_specs=pl.BlockSpec((hpt, MM_TM, ATT_HEAD_DIM), lambda j, i: (j, i, 0)),
        out_shape=jax.ShapeDtypeStruct((n // ATT_HEAD_DIM, s, ATT_HEAD_DIM), BF16),
        compiler_params=_cparams(("parallel", "parallel")),
        name="proj_qkv",
    )(u, w)


def _proj_kernel(u_ref, w_ref, o_ref):
    o_ref[...] = _dot(u_ref[...], w_ref[...]).astype(o_ref.dtype)


def _proj(u, w, col0, n, out_dtype, name):
    s, d = u.shape
    j0 = col0 // MM_TN
    return pl.pallas_call(
        _proj_kernel,
        grid=(n // MM_TN, s // MM_TM),
        in_specs=[
            pl.BlockSpec((MM_TM, d), lambda j, i: (i, 0)),
            pl.BlockSpec((d, MM_TN), lambda j, i: (0, j0 + j)),
        ],
        out_specs=pl.BlockSpec((MM_TM, MM_TN), lambda j, i: (i, j)),
        out_shape=jax.ShapeDtypeStruct((s, n), out_dtype),
        compiler_params=_cparams(("parallel", "parallel")),
        name=name,
    )(u, w)


def _attn_kernel(q_ref, k_ref, v_ref, nw_ref, uo_ref, o_ref, *scratch):
    acc_refs, c_refs = scratch[:ATT_HPS], scratch[ATT_HPS:]
    _attn_body(q_ref, k_ref, v_ref, nw_ref, uo_ref, o_ref, acc_refs, c_refs)


def _attn_body(q_ref, k_ref, v_ref, nw_ref, uo_ref, o_ref, acc_refs, c_refs):
    i = pl.program_id(1)
    uo = uo_ref[...]
    scale = 1.0 / math.sqrt(ATT_HEAD_DIM)
    row = lax.broadcasted_iota(I32, (ATT_BLK, ATT_BLK), 0)
    col = lax.broadcasted_iota(I32, (ATT_BLK, ATT_BLK), 1)
    strict = col < row

    def block(j, masked):
        off = pl.multiple_of(j * ATT_BLK, ATT_BLK)
        heads = range(ATT_HPS)
        zs = [_dot_nt(q_ref[hh], k_ref[hh, pl.ds(off, ATT_BLK), :]) * scale for hh in heads]
        sps = [_softplus(z) for z in zs]
        log_betas = [z - sp for z, sp in zip(zs, sps)]
        log_1ms = [jnp.where(strict, -sp, 0.0) if masked else -sp for sp in sps]
        splits = [_split2(x) for x in log_1ms]
        r2s = [_dot(hi, uo) + _dot(lo, uo) for hi, lo in splits]
        cmax = None
        ws = []
        for hh in heads:
            r = r2s[hh][:, :ATT_BLK]
            tot = r2s[hh][:, ATT_BLK:]
            if masked:
                ws.append(jnp.where(strict, jnp.exp(log_betas[hh] + r), 0.0).astype(BF16))
                c_new = tot
            else:
                c_old = c_refs[hh][...]
                ws.append(jnp.exp(log_betas[hh] + r + c_old).astype(BF16))
                c_new = c_old + tot
            c_refs[hh][...] = c_new
            cmax = c_new if cmax is None else jnp.maximum(cmax, c_new)
        for hh in heads:
            pv = _dot(ws[hh], v_ref[hh, pl.ds(off, ATT_BLK), :])
            if masked:
                acc_refs[hh][...] = pv
            else:
                acc_refs[hh][...] += pv
        return jnp.max(cmax)

    def cond(carry):
        return (carry[0] < i) & (carry[1] > -ATT_UNDERFLOW)

    def body(carry):
        t = carry[0]
        return t + 1, block(i - 1 - t, False)

    lax.while_loop(cond, body, (jnp.int32(0), block(i, True)))
    for hh in range(ATT_HPS):
        acc = acc_refs[hh][...]
        y = acc * lax.rsqrt(jnp.mean(acc * acc, axis=-1, keepdims=True) + EPS) * nw_ref[hh]
        o_ref[:, hh * ATT_HEAD_DIM:(hh + 1) * ATT_HEAD_DIM] = y.astype(o_ref.dtype)


def _attention(qkv, attn_norm_w, uo):
    s = qkv.shape[1]
    ng = ATT_HEADS // ATT_HPS
    return pl.pallas_call(
        _attn_kernel,
        grid=(ng, s // ATT_BLK),
        in_specs=[
            pl.BlockSpec((ATT_HPS, ATT_BLK, ATT_HEAD_DIM), lambda g, i: (g, i, 0)),
            pl.BlockSpec((ATT_HPS, s, ATT_HEAD_DIM), lambda g, i: (ng + g, 0, 0), pipeline_mode=pl.Buffered(1)),
            pl.BlockSpec((ATT_HPS, s, ATT_HEAD_DIM), lambda g, i: (2 * ng + g, 0, 0), pipeline_mode=pl.Buffered(1)),
            pl.BlockSpec((ATT_HPS, 1, ATT_HEAD_DIM), lambda g, i: (g, 0, 0)),
            pl.BlockSpec((ATT_BLK, 2 * ATT_BLK), lambda g, i: (0, 0)),
        ],
        out_specs=pl.BlockSpec((ATT_BLK, ATT_HPS * ATT_HEAD_DIM), lambda g, i: (i, g)),
        out_shape=jax.ShapeDtypeStruct((s, ATT_WIDTH), BF16),
        scratch_shapes=(
            [pltpu.VMEM((ATT_BLK, ATT_HEAD_DIM), F32) for _ in range(ATT_HPS)]
            + [pltpu.VMEM((ATT_BLK, ATT_BLK), F32) for _ in range(ATT_HPS)]
        ),
        compiler_params=_cparams(("parallel", "arbitrary")),
        name="sb_attn",
    )(qkv, qkv, qkv, attn_norm_w.reshape(ATT_HEADS, 1, ATT_HEAD_DIM), uo)


def _ssd_kernel(z_ref, xs_ref, bc_ref, dt_ref, cw_ref, cb_ref, dtb_ref, alog_ref, dskip_ref, nw_ref,
                tri_ref, eh_ref, et_ref, o_ref,
                xbuf, bbuf, xc, bcc, acol, acst, dtf, decf, expf, state):
    c = pl.program_id(0)
    t = SSD_T
    w2 = SSM_WIDTH

    @pl.when(c == 0)
    def _():
        xbuf[0:SUBLANES, :] = jnp.zeros((SUBLANES, w2), F32)
        bbuf[0:SUBLANES, :] = jnp.zeros((SUBLANES, w2), F32)
        state[...] = jnp.zeros_like(state)

    def conv_chunk(ci, carry):
        for src_ref, buf, dst, woff in ((xs_ref, xbuf, xc, 0), (bc_ref, bbuf, bcc, w2)):
            cols = pl.ds(pl.multiple_of(ci * SSD_CCH, SSD_CCH), SSD_CCH)
            wcols = pl.ds(pl.multiple_of(woff + ci * SSD_CCH, SSD_CCH), SSD_CCH)
            cur = src_ref[:, cols]
            buf[SUBLANES:SUBLANES + t, cols] = cur
            acc = cur * cw_ref[CONV_WIDTH - 1:CONV_WIDTH, wcols] + cb_ref[:, wcols]
            for k in range(CONV_WIDTH - 1):
                start = SUBLANES - (CONV_WIDTH - 1) + k
                acc = acc + buf[start:start + t, cols] * cw_ref[k:k + 1, wcols]
            dst[:, cols] = _silu(acc)
            buf[0:SUBLANES, cols] = cur[t - SUBLANES:, :]
        return carry

    lax.fori_loop(0, w2 // SSD_CCH, conv_chunk, 0)

    dt = _softplus(dt_ref[...] + dtb_ref[...])
    a = -jnp.exp(alog_ref[...])
    a_cs = _dot_exact01_lhs(tri_ref[...], dt * a)
    a_last = a_cs[t - 1:t, :]
    terms = []
    for v in (dt, dt * jnp.exp(a_last - a_cs), jnp.exp(a_cs)):
        terms.extend(_split2(v))
    ex = _dot(jnp.concatenate(terms, axis=0), eh_ref[...])
    dtf[...] = ex[0:t] + ex[t:2 * t]
    decf[...] = ex[2 * t:3 * t] + ex[3 * t:4 * t]
    expf[...] = ex[4 * t:5 * t] + ex[5 * t:6 * t]
    ac = _dot(jnp.concatenate(_split3(a_cs), axis=0), et_ref[...])
    acol[...] = ac[0:t] + ac[t:2 * t] + ac[2 * t:3 * t]
    acst[...] = a_cs.T

    row = lax.broadcasted_iota(I32, (t, t), 0)
    col = lax.broadcasted_iota(I32, (t, t), 1)
    lower = col <= row
    head_of_lane = lax.broadcasted_iota(I32, (t, SSM_GROUP_WIDTH), 1) // SSM_HEAD_DIM

    def group(g, carry):
        go = pl.multiple_of(g * SSM_GROUP_WIDTH, SSM_GROUP_WIDTH)
        no = pl.multiple_of(g * SSM_STATE, SSM_STATE)
        bg = bcc[:, pl.ds(no, SSM_STATE)]
        cg = bcc[:, pl.ds(SSM_GROUPS * SSM_STATE + no, SSM_STATE)]
        cgb = cg.astype(BF16)
        cb = _dot_nt(cgb, bg.astype(BF16))
        xg = xc[:, pl.ds(go, SSM_GROUP_WIDTH)]
        xd = xg * dtf[:, pl.ds(go, SSM_GROUP_WIDTH)]
        xdec = (xg * decf[:, pl.ds(go, SSM_GROUP_WIDTH)]).astype(BF16)
        st = state[g]
        y = _dot(cgb, st.astype(BF16)) * expf[:, pl.ds(go, SSM_GROUP_WIDTH)]
        for r in range(SSM_GROUP_WIDTH // SSM_HEAD_DIM):
            h = g * (SSM_GROUP_WIDTH // SSM_HEAD_DIM) + r
            ho = pl.multiple_of(h * t, t)
            seg = acol[:, pl.ds(ho, t)] - acst[pl.ds(h, 1), :]
            lmat = jnp.exp(jnp.where(lower, seg, NEG_BIG))
            gm = (cb * lmat).astype(BF16)
            xd_r = jnp.where(head_of_lane == r, xd, 0.0).astype(BF16)
            y = y + _dot(gm, xd_r)
        state[g] = st * expf[t - 1:t, pl.ds(go, SSM_GROUP_WIDTH)] + _dot(bg.T.astype(BF16), xdec)
        y = y + dskip_ref[:, pl.ds(go, SSM_GROUP_WIDTH)] * xg
        y = y * _silu(z_ref[:, pl.ds(go, SSM_GROUP_WIDTH)])
        y = y * lax.rsqrt(jnp.mean(y * y, axis=-1, keepdims=True) + EPS) * nw_ref[:, pl.ds(go, SSM_GROUP_WIDTH)]
        o_ref[:, pl.ds(go, SSM_GROUP_WIDTH)] = y.astype(o_ref.dtype)
        return carry

    lax.fori_loop(0, SSM_GROUPS, group, 0, unroll=8)


def _dot_exact01_lhs(m, x):
    h1, h2, h3 = _split3(x)
    return _dot(m, h1) + _dot(m, h2) + _dot(m, h3)


def _ssd(zxbc, dt_raw, conv_w, conv_b, dt_bias, a_log, d_skip, ssd_norm_w):
    s = zxbc.shape[0]
    t = SSD_T
    w2 = SSM_WIDTH
    pad = LANES - SSM_HEADS
    tri = (jnp.arange(t)[:, None] >= jnp.arange(t)[None, :]).astype(BF16)
    heads = jnp.arange(LANES)[:, None]
    eh = (heads == (jnp.arange(w2)[None, :] // SSM_HEAD_DIM)).astype(BF16)
    et = (heads == (jnp.arange(SSM_HEADS * t)[None, :] // t)).astype(BF16)
    dtb = jnp.pad(dt_bias.astype(F32), (0, pad)).reshape(1, LANES)
    alog = jnp.pad(a_log.astype(F32), (0, pad)).reshape(1, LANES)
    dskip = jnp.repeat(d_skip.astype(F32), SSM_HEAD_DIM).reshape(1, w2)
    const = lambda c: (0, 0)
    return pl.pallas_call(
        _ssd_kernel,
        grid=(s // t,),
        in_specs=[
            pl.BlockSpec((t, w2), lambda c: (c, 0)),
            pl.BlockSpec((t, w2), lambda c: (c, 1)),
            pl.BlockSpec((t, w2), lambda c: (c, 2)),
            pl.BlockSpec((t, LANES), lambda c: (c, 0)),
            pl.BlockSpec((CONV_WIDTH, 2 * w2), const),
            pl.BlockSpec((1, 2 * w2), const),
            pl.BlockSpec((1, LANES), const),
            pl.BlockSpec((1, LANES), const),
            pl.BlockSpec((1, w2), const),
            pl.BlockSpec((1, w2), const),
            pl.BlockSpec((t, t), const),
            pl.BlockSpec((LANES, w2), const),
            pl.BlockSpec((LANES, SSM_HEADS * t), const),
        ],
        out_specs=pl.BlockSpec((t, w2), lambda c: (c, 0)),
        out_shape=jax.ShapeDtypeStruct((s, w2), BF16),
        scratch_shapes=[
            pltpu.VMEM((t + SUBLANES, w2), F32),
            pltpu.VMEM((t + SUBLANES, w2), F32),
            pltpu.VMEM((t, w2), F32),
            pltpu.VMEM((t, w2), F32),
            pltpu.VMEM((t, SSM_HEADS * t), F32),
            pltpu.VMEM((LANES, t), F32),
            pltpu.VMEM((t, w2), F32),
            pltpu.VMEM((t, w2), F32),
            pltpu.VMEM((t, w2), F32),
            pltpu.VMEM((SSM_GROUPS, SSM_STATE, SSM_GROUP_WIDTH), F32),
        ],
        compiler_params=_cparams(("arbitrary",)),
        name="ssd",
    )(zxbc, zxbc, zxbc, dt_raw, conv_w.astype(F32), conv_b.astype(F32).reshape(1, 2 * w2), dtb, alog, dskip,
      ssd_norm_w.astype(F32).reshape(1, w2), tri, eh, et)


def _out_proj_kernel(att_ref, y_ref, w_ref, x_ref, o_ref):
    acc = _dot(att_ref[...], w_ref[0:ATT_WIDTH, :])
    acc = acc + _dot(y_ref[...], w_ref[ATT_WIDTH:, :])
    o_ref[...] = x_ref[...] + acc


def _out_proj(att, y, w_b, x2):
    s, d = x2.shape
    tm, tn = OUT_TM, OUT_TN
    return pl.pallas_call(
        _out_proj_kernel,
        grid=(d // tn, s // tm),
        in_specs=[
            pl.BlockSpec((tm, ATT_WIDTH), lambda j, i: (i, 0)),
            pl.BlockSpec((tm, SSM_WIDTH), lambda j, i: (i, 0)),
            pl.BlockSpec((ATT_WIDTH + SSM_WIDTH, tn), lambda j, i: (0, j)),
            pl.BlockSpec((tm, tn), lambda j, i: (i, j)),
        ],
        out_specs=pl.BlockSpec((tm, tn), lambda j, i: (i, j)),
        out_shape=jax.ShapeDtypeStruct((s, d), F32),
        compiler_params=_cparams(("parallel", "parallel")),
        name="out_proj",
    )(att, y, w_b, x2)


def _router_kernel(h_ref, w_ref, whi_ref, wlo_ref, b_ref, up_ref, route_ref):
    x = h_ref[...]
    u = x * lax.rsqrt(jnp.mean(x * x, axis=-1, keepdims=True) + EPS) * w_ref[...]
    half = D_MODEL // 2
    up_ref[...] = _pack_bf16_pair(u[:, :half], u[:, half:])
    uh, ul = _split2(u)
    whi = whi_ref[...]
    logits = _dot(uh, whi) + _dot(ul, whi) + _dot(uh, wlo_ref[...]) + b_ref[...]
    lane = lax.broadcasted_iota(I32, logits.shape, 1).astype(F32)
    ninf = -jnp.inf
    nolane = float(LANES)
    gl = jnp.where(lane < MOE_GROUPS, logits, ninf)
    gmax = jnp.max(gl, axis=-1, keepdims=True)
    gsum = jnp.sum(jnp.exp(gl - gmax), axis=-1, keepdims=True)
    g_w = 1.0 / gsum
    g_idx = jnp.min(jnp.where(gl == gmax, lane, nolane), axis=-1, keepdims=True)
    lo_lane = MOE_GROUPS + g_idx * EXPERTS_PER_GROUP
    el = jnp.where((lane >= lo_lane) & (lane < lo_lane + EXPERTS_PER_GROUP), logits, ninf)
    m0 = jnp.max(el, axis=-1, keepdims=True)
    i0 = jnp.min(jnp.where(el == m0, lane, nolane), axis=-1, keepdims=True)
    el2 = jnp.where(lane == i0, ninf, el)
    m1 = jnp.max(el2, axis=-1, keepdims=True)
    i1 = jnp.min(jnp.where(el2 == m1, lane, nolane), axis=-1, keepdims=True)
    tt = jnp.exp(m1 - m0)
    w0 = g_w / (1.0 + tt)
    w1 = g_w * tt / (1.0 + tt)
    e0 = i0 - MOE_GROUPS
    e1 = i1 - MOE_GROUPS
    out = jnp.where(lane == 0, e0, jnp.where(lane == 1, e1, jnp.where(lane == 2, w0, jnp.where(lane == 3, w1, 0.0))))
    route_ref[...] = out


def _router(h1, norm_w, wr_hi, wr_lo, br):
    s, d = h1.shape
    return pl.pallas_call(
        _router_kernel,
        grid=(s // NORM_ROWS,),
        in_specs=[
            pl.BlockSpec((NORM_ROWS, d), lambda i: (i, 0)),
            pl.BlockSpec((1, d), lambda i: (0, 0)),
            pl.BlockSpec((d, LANES), lambda i: (0, 0)),
            pl.BlockSpec((d, LANES), lambda i: (0, 0)),
            pl.BlockSpec((1, LANES), lambda i: (0, 0)),
        ],
        out_specs=[
            pl.BlockSpec((NORM_ROWS, d // 2), lambda i: (i, 0)),
            pl.BlockSpec((NORM_ROWS, LANES), lambda i: (i, 0)),
        ],
        out_shape=[jax.ShapeDtypeStruct((s, d // 2), U32), jax.ShapeDtypeStruct((s, LANES), F32)],
        compiler_params=_cparams(("parallel",)),
        name="router",
    )(h1, norm_w.reshape(1, d), wr_hi, wr_lo, br)


def _gather_rows_kernel(idx_ref, nvalid_ref, src_ref, o_ref, sem):
    base = pl.program_id(0) * GATHER_BLK

    def copy(r):
        return pltpu.make_async_copy(src_ref.at[pl.ds(idx_ref[base + r], 1)], o_ref.at[pl.ds(r, 1)], sem)

    @pl.when(base < nvalid_ref[0])
    def _():
        def issue(g, carry):
            for k in range(GATHER_UNROLL):
                copy(g * GATHER_UNROLL + k).start(priority=k % 2)
            return carry

        lax.fori_loop(0, GATHER_BLK // GATHER_UNROLL, issue, 0)

        def drain(r, carry):
            copy(r).wait()
            return carry

        lax.fori_loop(0, GATHER_BLK, drain, 0, unroll=GATHER_UNROLL)

    @pl.when(base >= nvalid_ref[0])
    def _():
        o_ref[...] = jnp.zeros_like(o_ref)


def _gather_rows(idx, nvalid, src, n_out, name):
    return pl.pallas_call(
        _gather_rows_kernel,
        grid_spec=pltpu.PrefetchScalarGridSpec(
            num_scalar_prefetch=2,
            grid=(n_out // GATHER_BLK,),
            in_specs=[pl.BlockSpec(memory_space=pl.ANY)],
            out_specs=pl.BlockSpec((GATHER_BLK, src.shape[1]), lambda b, idx_ref, nv_ref: (b, 0)),
            scratch_shapes=[pltpu.SemaphoreType.DMA(())],
        ),
        out_shape=jax.ShapeDtypeStruct((n_out, src.shape[1]), src.dtype),
        compiler_params=_cparams(("arbitrary",)),
        name=name,
    )(idx, nvalid, src)


def _zero_fill_blocks(zbuf, dst_hbm, first_blk, sem):
    zbuf[...] = jnp.zeros_like(zbuf)

    def copy(b):
        r0 = pl.multiple_of(b * MOE_BLK, MOE_BLK)
        return pltpu.make_async_copy(zbuf, dst_hbm.at[pl.ds(r0, MOE_BLK)], sem)

    def start(b, carry):
        copy(b).start()
        return carry

    def wait(b, carry):
        copy(b).wait()
        return carry

    lax.fori_loop(first_blk, MOE_NBLK, start, 0)
    lax.fori_loop(first_blk, MOE_NBLK, wait, 0)


def _walk_blocks(n, n_next, is_first, is_last, in_copy, next_first_copy, out_copy, compute, flags):
    @pl.when(is_first)
    def _():
        flags[0] = 0
        flags[1] = 0
        flags[2] = 0

    @pl.when((n > 0) & (flags[2] == 0))
    def _():
        in_copy(0, 0).start(priority=BLOCK_DMA_PRIORITY)

    def body(j, carry):
        slot = j % 2
        in_copy(j, slot).wait()

        @pl.when(j + 1 < n)
        def _():
            in_copy(j + 1, 1 - slot).start(priority=BLOCK_DMA_PRIORITY)

        @pl.when(flags[slot] == 1)
        def _():
            out_copy(j, slot).wait()

        compute(slot)
        out_copy(j, slot).start()
        flags[slot] = 1
        return carry

    lax.fori_loop(0, n, body, 0)

    start_next = jnp.logical_and(jnp.logical_not(is_last), n_next > 0)

    @pl.when(start_next)
    def _():
        next_first_copy().start(priority=BLOCK_DMA_PRIORITY)

    flags[2] = start_next.astype(I32)

    @pl.when(is_last)
    def _():
        for slot in range(2):
            @pl.when(flags[slot] == 1)
            def _():
                out_copy(0, slot).wait()


def _moe_up_kernel(first_ref, cnt_ref, nact_ref, xs_hbm, wg_ref, wu_ref, hdn_hbm,
                   wgb, wub, xbuf, hbuf, zbuf, in_sem, out_sem, z_sem, flags):
    e = pl.program_id(0)
    ht = pl.program_id(1)
    nht = pl.num_programs(1)
    n = cnt_ref[e]
    b0 = first_ref[e]
    col0 = pl.multiple_of(ht * MOE_TH, LANES)
    e_next = jnp.minimum(jnp.where(ht + 1 < nht, e, e + 1), N_EXPERTS - 1)
    is_last = (e == N_EXPERTS - 1) & (ht == nht - 1)

    def block_rows(blk):
        return pl.ds(pl.multiple_of(blk * MOE_BLK, MOE_BLK), MOE_BLK)

    def rows(j):
        return block_rows(b0 + j)

    def in_copy(j, slot):
        return pltpu.make_async_copy(xs_hbm.at[rows(j)], xbuf.at[slot], in_sem.at[slot])

    def next_first_copy():
        return pltpu.make_async_copy(xs_hbm.at[block_rows(first_ref[e_next])], xbuf.at[0], in_sem.at[0])

    def out_copy(j, slot):
        return pltpu.make_async_copy(hbuf.at[slot], hdn_hbm.at[rows(j), pl.ds(col0, MOE_TH)], out_sem.at[slot])

    @pl.when((e == 0) & (ht == 0))
    def _():
        _zero_fill_blocks(zbuf, hdn_hbm, nact_ref[0], z_sem)

    @pl.when(n > 0)
    def _():
        wgb[...] = wg_ref[0].astype(BF16)
        wub[...] = wu_ref[0].astype(BF16)

    def compute(slot):
        xa, xb = _unpack_bf16_pair(xbuf[slot])
        x = jnp.concatenate([xa.astype(BF16), xb.astype(BF16)], axis=-1)
        g = _dot(x, wgb[...])
        u = _dot(x, wub[...])
        hbuf[slot] = (_silu(g) * u).astype(BF16)

    _walk_blocks(n, cnt_ref[e_next], (e == 0) & (ht == 0), is_last, in_copy, next_first_copy, out_copy, compute, flags)


def _moe_up(blk_first, blk_cnt, nact, xs, w_gate, w_up):
    nht = EXPERT_HIDDEN // MOE_TH
    half = D_MODEL // 2
    return pl.pallas_call(
        _moe_up_kernel,
        grid_spec=pltpu.PrefetchScalarGridSpec(
            num_scalar_prefetch=3,
            grid=(N_EXPERTS, nht),
            in_specs=[
                pl.BlockSpec(memory_space=pl.ANY),
                pl.BlockSpec((1, D_MODEL, MOE_TH), lambda e, ht, bf, bc, na: (e, 0, ht)),
                pl.BlockSpec((1, D_MODEL, MOE_TH), lambda e, ht, bf, bc, na: (e, 0, ht)),
            ],
            out_specs=pl.BlockSpec(memory_space=pl.ANY),
            scratch_shapes=[
                pltpu.VMEM((D_MODEL, MOE_TH), BF16),
                pltpu.VMEM((D_MODEL, MOE_TH), BF16),
                pltpu.VMEM((2, MOE_BLK, half), U32),
                pltpu.VMEM((2, MOE_BLK, MOE_TH), BF16),
                pltpu.VMEM((MOE_BLK, EXPERT_HIDDEN), BF16),
                pltpu.SemaphoreType.DMA((2,)),
                pltpu.SemaphoreType.DMA((2,)),
                pltpu.SemaphoreType.DMA(()),
                pltpu.SMEM((3,), I32),
            ],
        ),
        out_shape=jax.ShapeDtypeStruct((MOE_ROWS, EXPERT_HIDDEN), BF16),
        compiler_params=_cparams(("arbitrary", "arbitrary")),
        name="moe_up",
    )(blk_first, blk_cnt, nact, xs, w_gate, w_up)


def _moe_down_kernel(first_ref, cnt_ref, nact_ref, hdn_hbm, wd_ref, ys_hbm,
                     wdb, hbuf, ybuf, zbuf, in_sem, out_sem, z_sem, flags):
    e = pl.program_id(0)
    n = cnt_ref[e]
    b0 = first_ref[e]
    half = D_MODEL // 2
    e_next = jnp.minimum(e + 1, N_EXPERTS - 1)

    def block_rows(blk):
        return pl.ds(pl.multiple_of(blk * MOE_BLK, MOE_BLK), MOE_BLK)

    def rows(j):
        return block_rows(b0 + j)

    def in_copy(j, slot):
        return pltpu.make_async_copy(hdn_hbm.at[rows(j)], hbuf.at[slot], in_sem.at[slot])

    def next_first_copy():
        return pltpu.make_async_copy(hdn_hbm.at[block_rows(first_ref[e_next])], hbuf.at[0], in_sem.at[0])

    def out_copy(j, slot):
        return pltpu.make_async_copy(ybuf.at[slot], ys_hbm.at[rows(j)], out_sem.at[slot])

    @pl.when(e == 0)
    def _():
        _zero_fill_blocks(zbuf, ys_hbm, nact_ref[0], z_sem)

    @pl.when(n > 0)
    def _():
        wdb[...] = wd_ref[0].astype(BF16)

    def compute(slot):
        y = _dot(hbuf[slot], wdb[...])
        ybuf[slot] = _pack_bf16_pair(y[:, :half], y[:, half:])

    _walk_blocks(n, cnt_ref[e_next], e == 0, e == N_EXPERTS - 1, in_copy, next_first_copy, out_copy, compute, flags)


def _moe_down(blk_first, blk_cnt, nact, hdn, w_down):
    half = D_MODEL // 2
    return pl.pallas_call(
        _moe_down_kernel,
        grid_spec=pltpu.PrefetchScalarGridSpec(
            num_scalar_prefetch=3,
            grid=(N_EXPERTS,),
            in_specs=[
                pl.BlockSpec(memory_space=pl.ANY),
                pl.BlockSpec((1, EXPERT_HIDDEN, D_MODEL), lambda e, bf, bc, na: (e, 0, 0)),
            ],
            out_specs=pl.BlockSpec(memory_space=pl.ANY),
            scratch_shapes=[
                pltpu.VMEM((EXPERT_HIDDEN, D_MODEL), BF16),
                pltpu.VMEM((2, MOE_BLK, EXPERT_HIDDEN), BF16),
                pltpu.VMEM((2, MOE_BLK, half), U32),
                pltpu.VMEM((MOE_BLK, half), U32),
                pltpu.SemaphoreType.DMA((2,)),
                pltpu.SemaphoreType.DMA((2,)),
                pltpu.SemaphoreType.DMA(()),
                pltpu.SMEM((3,), I32),
            ],
        ),
        out_shape=jax.ShapeDtypeStruct((MOE_ROWS, half), U32),
        compiler_params=_cparams(("arbitrary",)),
        name="moe_down",
    )(blk_first, blk_cnt, nact, hdn, w_down)


def _combine_kernel(h_ref, ya0_ref, ya1_ref, route_ref, w_ref, o_ref):
    route = route_ref[...]
    w0 = route[:, 2:3]
    w1 = route[:, 3:4]
    a0, b0 = _unpack_bf16_pair(ya0_ref[...])
    a1, b1 = _unpack_bf16_pair(ya1_ref[...])
    moe = jnp.concatenate([w0 * a0 + w1 * a1, w0 * b0 + w1 * b1], axis=-1)
    h = h_ref[...] + moe
    o_ref[...] = h * lax.rsqrt(jnp.mean(h * h, axis=-1, keepdims=True) + EPS) * w_ref[...]


def _combine(h1, ya, route, norm_w):
    s, d = h1.shape
    nb = s // NORM_ROWS
    return pl.pallas_call(
        _combine_kernel,
        grid=(nb,),
        in_specs=[
            pl.BlockSpec((NORM_ROWS, d), lambda i: (i, 0)),
            pl.BlockSpec((NORM_ROWS, d // 2), lambda i: (i, 0)),
            pl.BlockSpec((NORM_ROWS, d // 2), lambda i: (i + nb, 0)),
            pl.BlockSpec((NORM_ROWS, LANES), lambda i: (i, 0)),
            pl.BlockSpec((1, d), lambda i: (0, 0)),
        ],
        out_specs=pl.BlockSpec((NORM_ROWS, d), lambda i: (i, 0)),
        out_shape=jax.ShapeDtypeStruct((s, d), F32),
        compiler_params=_cparams(("parallel",)),
        name="combine",
    )(h1, ya, ya, route, norm_w.reshape(1, d))


def _dispatch_plan(route):
    s = route.shape[0]
    flat_e = route[:, :EXPERT_TOP_K].astype(I32).reshape(-1)
    onehot = (flat_e[:, None] == jnp.arange(N_EXPERTS, dtype=I32)[None, :]).astype(I32)
    csum = jnp.cumsum(onehot, axis=0)
    rank = jnp.sum((csum - 1) * onehot, axis=1)
    counts = csum[-1]
    padded = (counts + MOE_BLK - 1) // MOE_BLK * MOE_BLK
    pend = jnp.cumsum(padded)
    pstart = pend - padded
    dest = (pstart[flat_e] + rank).astype(I32)
    flat_t = jnp.repeat(jnp.arange(s, dtype=I32), EXPERT_TOP_K)
    row_tok = (jnp.arange(MOE_ROWS, dtype=I32) % s).at[dest].set(flat_t, unique_indices=True)
    nact = (pend[-1] // MOE_BLK).astype(I32).reshape(1)
    blk_first = (pstart // MOE_BLK).astype(I32)
    blk_cnt = (padded // MOE_BLK).astype(I32)
    return dest, row_tok, blk_first, blk_cnt, nact


def kernel(x, norm_mix_w, w_in, conv_w, conv_b, dt_bias, a_log, d_skip, ssd_norm_w, attn_norm_w, w_out,
           norm_ffn_w, w_group, b_group, w_expert, b_expert, w_gate, w_up, w_down, norm_final_w):
    b, s, d = x.shape
    x2 = x.reshape(b * s, d)
    n_qkv = 3 * ATT_WIDTH
    n_zxbc = SSM_WIDTH + SSM_WIDTH + 2 * SSM_GROUPS * SSM_STATE

    w_in_b = w_in.astype(BF16)
    w_dt = jnp.pad(w_in_b[:, n_qkv + n_zxbc:], ((0, 0), (0, LANES - SSM_HEADS)))
    w_out_b = w_out.astype(BF16)
    wr = jnp.concatenate(
        [w_group, jnp.transpose(w_expert, (1, 0, 2)).reshape(d, N_EXPERTS),
         jnp.zeros((d, LANES - MOE_GROUPS - N_EXPERTS), F32)], axis=1)
    wr_hi = wr.astype(BF16)
    wr_lo = (wr - wr_hi.astype(F32)).astype(BF16)
    br = jnp.concatenate([b_group, b_expert.reshape(-1), jnp.zeros((LANES - MOE_GROUPS - N_EXPERTS,), F32)]).reshape(1, LANES)
    kk = jnp.arange(ATT_BLK)
    uo = jnp.concatenate([(kk[:, None] > kk[None, :]).astype(BF16), jnp.ones((ATT_BLK, ATT_BLK), BF16)], axis=1)

    u, dt_raw = _norm_dt(x2, norm_mix_w, w_dt)
    qkv = _proj_heads(u, w_in_b, 0, n_qkv)
    zxbc = _proj(u, w_in_b, n_qkv, n_zxbc, F32, "proj_zxbc")
    att = _attention(qkv, attn_norm_w, uo)
    y = _ssd(zxbc, dt_raw, conv_w, conv_b, dt_bias, a_log, d_skip, ssd_norm_w)
    h1 = _out_proj(att, y, w_out_b, x2)

    up, route = _router(h1, norm_ffn_w, wr_hi, wr_lo, br)
    dest, row_tok, blk_first, blk_cnt, nact = _dispatch_plan(route)
    xs = _gather_rows(row_tok, nact * MOE_BLK, up, MOE_ROWS, "dispatch")
    hdn = _moe_up(blk_first, blk_cnt, nact, xs, w_gate, w_up)
    ys = _moe_down(blk_first, blk_cnt, nact, hdn, w_down)
    slot_major = dest.reshape(s, EXPERT_TOP_K).T.reshape(-1)
    n_slots = jnp.full((1,), s * EXPERT_TOP_K, I32)
    ya = _gather_rows(slot_major, n_slots, ys, s * EXPERT_TOP_K, "undispatch")
    out = _combine(h1, ya, route, norm_final_w)
    return out.reshape(b, s, d)
```
